```python
import jax
import jax.numpy as jnp
from jax import lax
import numpy as np

D_MODEL = 1024
BATCH = 8
SEQ = 4096
DEPTH = 2

EPS = 1e-6
MEM_LEN = 256
GDN_HEADS = 6
GDN_DK = 128
GDN_DV = 128
GDN_WIDTH = GDN_HEADS * GDN_DV
CONV_K = 4
CHUNK = 64
SB_HEADS = 12
SB_DH = 64
SB_WIDTH = SB_HEADS * SB_DH
SB_BLOCK = 128
MEM_HEADS = 4
MEM_DH = 64
MEM_WIDTH = MEM_HEADS * MEM_DH
MIX_WIDTH = GDN_WIDTH + MEM_WIDTH
A_IN = 4 * GDN_WIDTH + 2 * GDN_HEADS + MEM_WIDTH
B_IN = SB_WIDTH + MEM_WIDTH
N_GROUPS = 4
EXPERTS_PER_GROUP = 4
N_EXPERTS = N_GROUPS * EXPERTS_PER_GROUP
TOP_K = 2
EXPERT_FF = 256

kernel_name = 'yoco_gdn_stickbreak_hmoe'


def rmsnorm(x, g):
    xf = x.astype(jnp.float32)
    y = xf * lax.rsqrt(jnp.mean(xf * xf, axis=-1, keepdims=True) + EPS)
    return (y * g.astype(jnp.float32)).astype(x.dtype)


def l2norm(x):
    return x * lax.rsqrt(jnp.sum(x * x, axis=-1, keepdims=True) + EPS)


def causal_dwconv(x, w):
    c = x.shape[-1]
    return lax.conv_general_dilated(
        x, w[:, None, :].astype(x.dtype), window_strides=(1,),
        padding=((CONV_K - 1, 0),), dimension_numbers=('NWC', 'WIO', 'NWC'),
        feature_group_count=c)


def gated_delta_rule(q, k, v, beta, g):
    bsz, s, h, _ = q.shape
    n = s // CHUNK

    def chunks(t):
        return t.reshape(bsz, n, CHUNK, h, -1).transpose(0, 3, 1, 2, 4)

    q, k, v = chunks(q), chunks(k), chunks(v)
    beta = beta.reshape(bsz, n, CHUNK, h).transpose(0, 3, 1, 2)
    gc = jnp.cumsum(g.reshape(bsz, n, CHUNK, h).transpose(0, 3, 1, 2), axis=-1)
    idx = jnp.arange(CHUNK)
    incl = idx[:, None] >= idx[None, :]
    strict = idx[:, None] > idx[None, :]
    decay = jnp.exp(jnp.where(incl, gc[..., :, None] - gc[..., None, :], -jnp.inf))
    kb = k * beta[..., None]
    a_mat = jnp.where(strict, jnp.einsum('bhnid,bhnjd->bhnij', kb, k) * decay, 0.0)
    eye = jnp.eye(CHUNK, dtype=q.dtype)
    rhs = jnp.concatenate([v * beta[..., None], kb * jnp.exp(gc)[..., None]], axis=-1)
    sol = lax.linalg.triangular_solve(eye + a_mat, rhs, left_side=True, lower=True)
    u, w = sol[..., :GDN_DV], sol[..., GDN_DV:]
    qg = q * jnp.exp(gc)[..., None]
    attn_intra = jnp.where(incl, jnp.einsum('bhnid,bhnjd->bhnij', q, k) * decay, 0.0)
    g_last = gc[..., -1]
    k_tail = k * jnp.exp(g_last[..., None] - gc)[..., None]

    def step(state, xs):
        qg_c, w_c, u_c, att_c, kt_c, gl_c = xs
        v_new = u_c - jnp.einsum('bhck,bhkv->bhcv', w_c, state)
        o = jnp.einsum('bhck,bhkv->bhcv', qg_c, state) + jnp.einsum('bhij,bhjv->bhiv', att_c, v_new)
        state = state * jnp.exp(gl_c)[..., None, None] + jnp.einsum('bhck,bhcv->bhkv', kt_c, v_new)
        return state, o

    xs = tuple(jnp.moveaxis(t, 2, 0) for t in (qg, w, u, attn_intra, k_tail, g_last))
    s0 = jnp.zeros((bsz, h, GDN_DK, GDN_DV), q.dtype)
    _, o = lax.scan(step, s0, xs)
    return o.transpose(1, 0, 3, 2, 4).reshape(bsz, s, h, GDN_DV)


def memory_attend(mq, mem_k, mem_v):
    bsz, s, _ = mq.shape
    q = mq.reshape(bsz, s, MEM_HEADS, MEM_DH)
    sc = jnp.einsum('bshd,bmhd->bhsm', q, mem_k).astype(jnp.float32) * (MEM_DH ** -0.5)
    p = jax.nn.softmax(sc, axis=-1).astype(mem_v.dtype)
    return jnp.einsum('bhsm,bmhd->bshd', p, mem_v).reshape(bsz, s, MEM_WIDTH)


def stick_breaking(q, k, v):
    s_len = q.shape[2]
    outs = []
    for i in range(s_len // SB_BLOCK):
        hi = (i + 1) * SB_BLOCK
        qb = q[:, :, i * SB_BLOCK:hi]
        kb = k[:, :, :hi]
        vb = v[:, :, :hi]
        z = jnp.einsum('bhtd,bhsd->bhts', qb, kb).astype(jnp.float32) * (SB_DH ** -0.5)
        t_pos = i * SB_BLOCK + jnp.arange(SB_BLOCK)
        causal = jnp.arange(hi)[None, :] < t_pos[:, None]
        log_1mb = jnp.where(causal, jax.nn.log_sigmoid(-z), 0.0)
        after = lax.cumsum(log_1mb, axis=3, reverse=True) - log_1mb
        a = jnp.where(causal, jnp.exp(jax.nn.log_sigmoid(z) + after), 0.0)
        outs.append(jnp.einsum('bhts,bhsd->bhtd', a.astype(vb.dtype), vb))
    return jnp.concatenate(outs, axis=2)


def mixer_a(h, norm_g, w_in, w_conv, a_log, dt_bias, o_gain, w_out, mem_k, mem_v):
    bsz, s, _ = h.shape
    f32 = jnp.float32
    p = rmsnorm(h, norm_g) @ w_in
    qkv = jax.nn.silu(causal_dwconv(p[..., :3 * GDN_WIDTH], w_conv)).astype(f32)
    gate = p[..., 3 * GDN_WIDTH:4 * GDN_WIDTH].astype(f32).reshape(bsz, s, GDN_HEADS, GDN_DV)
    b_raw = p[..., 4 * GDN_WIDTH:4 * GDN_WIDTH + GDN_HEADS].astype(f32)
    a_raw = p[..., 4 * GDN_WIDTH + GDN_HEADS:4 * GDN_WIDTH + 2 * GDN_HEADS].astype(f32)
    mq = p[..., 4 * GDN_WIDTH + 2 * GDN_HEADS:]
    q = l2norm(qkv[..., :GDN_WIDTH].reshape(bsz, s, GDN_HEADS, GDN_DK)) * (GDN_DK ** -0.5)
    k = l2norm(qkv[..., GDN_WIDTH:2 * GDN_WIDTH].reshape(bsz, s, GDN_HEADS, GDN_DK))
    v = qkv[..., 2 * GDN_WIDTH:].reshape(bsz, s, GDN_HEADS, GDN_DV)
    beta = jax.nn.sigmoid(b_raw)
    g = -jnp.exp(a_log.astype(f32)) * jax.nn.softplus(a_raw + dt_bias.astype(f32))
    o = gated_delta_rule(q, k, v, beta, g)
    o = o * lax.rsqrt(jnp.mean(o * o, axis=-1, keepdims=True) + EPS) * o_gain.astype(f32) * jax.nn.silu(gate)
    o = o.reshape(bsz, s, GDN_WIDTH).astype(h.dtype)
    m = memory_attend(mq, mem_k, mem_v)
    return jnp.concatenate([o, m], axis=-1) @ w_out


def mixer_b(h, norm_g, w_in, w_out, k_sh, v_sh, mem_k, mem_v):
    bsz, s, _ = h.shape
    p = rmsnorm(h, norm_g) @ w_in
    q = p[..., :SB_WIDTH].reshape(bsz, s, SB_HEADS, SB_DH).transpose(0, 2, 1, 3)
    o = stick_breaking(q, k_sh, v_sh).transpose(0, 2, 1, 3).reshape(bsz, s, SB_WIDTH)
    m = memory_attend(p[..., SB_WIDTH:], mem_k, mem_v)
    return jnp.concatenate([o.astype(h.dtype), m], axis=-1) @ w_out


def hier_moe(xn, w_group, b_group, w_router, b_router, w1, w3, w2):
    bsz, s, d = xn.shape
    t = xn.reshape(-1, d)
    n_tok = t.shape[0]
    gl = (t @ w_group).astype(jnp.float32) + b_group.astype(jnp.float32)
    gsel = jnp.argmax(gl, axis=-1)
    p_group = jnp.max(jax.nn.softmax(gl, axis=-1), axis=-1, keepdims=True)
    el = ((t @ w_router).astype(jnp.float32) + b_router.astype(jnp.float32)).reshape(n_tok, N_GROUPS, EXPERTS_PER_GROUP)
    el = el[jnp.arange(n_tok), gsel]
    top_p, top_i = lax.top_k(jax.nn.softmax(el, axis=-1), TOP_K)
    top_p = top_p / jnp.sum(top_p, axis=-1, keepdims=True)
    eid = gsel[:, None] * EXPERTS_PER_GROUP + top_i
    combine = jnp.sum(jax.nn.one_hot(eid, N_EXPERTS, dtype=jnp.float32) * (p_group * top_p)[..., None], axis=1)
    combine = combine.astype(t.dtype)
    y = jnp.zeros_like(t)
    for e in range(N_EXPERTS):
        hid = jax.nn.silu(t @ w1[e]) * (t @ w3[e])
        y = y + combine[:, e:e + 1] * (hid @ w2[e])
    return y.reshape(bsz, s, d)


def setup_inputs(seed: int = 0) -> dict:
    key = jax.random.key(seed)
    ks = iter(jax.random.split(key, 32))
    n_a = DEPTH // 2
    n_b = DEPTH - n_a

    def nrm(shape, scale):
        return jax.random.normal(next(ks), shape, jnp.float32) * scale

    def gain(shape):
        return 1.0 + nrm(shape, 0.02)

    x = nrm((BATCH, SEQ, D_MODEL), 1.0)
    mem = nrm((BATCH, MEM_LEN, D_MODEL), 1.0)
    a_norm = gain((n_a, D_MODEL))
    a_w_in = nrm((n_a, D_MODEL, A_IN), D_MODEL ** -0.5)
    a_conv = nrm((n_a, CONV_K, 3 * GDN_WIDTH), CONV_K ** -0.5)
    a_log = jnp.log(jax.random.uniform(next(ks), (n_a, GDN_HEADS), jnp.float32, 1.0, 16.0))
    dt = jnp.exp(jax.random.uniform(next(ks), (n_a, GDN_HEADS), jnp.float32, np.log(1e-3), np.log(1e-1)))
    a_dt_bias = dt + jnp.log(-jnp.expm1(-dt))
    a_out_gain = gain((n_a, GDN_DV))
    a_w_out = nrm((n_a, MIX_WIDTH, D_MODEL), MIX_WIDTH ** -0.5)
    kv_norm = gain((D_MODEL,))
    w_kv = nrm((D_MODEL, 2 * SB_WIDTH), D_MODEL ** -0.5)
    b_norm = gain((n_b, D_MODEL))
    b_w_in = nrm((n_b, D_MODEL, B_IN), D_MODEL ** -0.5)
    b_w_out = nrm((n_b, MIX_WIDTH, D_MODEL), MIX_WIDTH ** -0.5)
    mem_norm = gain((DEPTH, D_MODEL))
    w_mem_kv = nrm((DEPTH, D_MODEL, 2 * MEM_WIDTH), D_MODEL ** -0.5)
    ffn_norm = gain((DEPTH, D_MODEL))
    w_group = nrm((DEPTH, D_MODEL, N_GROUPS), D_MODEL ** -0.5)
    b_group = nrm((DEPTH, N_GROUPS), 0.01)
    w_router = nrm((DEPTH, D_MODEL, N_EXPERTS), D_MODEL ** -0.5)
    b_router = nrm((DEPTH, N_EXPERTS), 0.01)
    w1 = nrm((DEPTH, N_EXPERTS, D_MODEL, EXPERT_FF), D_MODEL ** -0.5)
    w3 = nrm((DEPTH, N_EXPERTS, D_MODEL, EXPERT_FF), D_MODEL ** -0.5)
    w2 = nrm((DEPTH, N_EXPERTS, EXPERT_FF, D_MODEL), EXPERT_FF ** -0.5)
    final_norm = gain((D_MODEL,))
    return {'x': x, 'mem': mem, 'a_norm': a_norm, 'a_w_in': a_w_in, 'a_conv': a_conv,
            'a_log': a_log, 'a_dt_bias': a_dt_bias, 'a_out_gain': a_out_gain, 'a_w_out': a_w_out,
            'kv_norm': kv_norm, 'w_kv': w_kv, 'b_norm': b_norm, 'b_w_in': b_w_in, 'b_w_out': b_w_out,
            'mem_norm': mem_norm, 'w_mem_kv': w_mem_kv, 'ffn_norm': ffn_norm,
            'w_group': w_group, 'b_group': b_group, 'w_router': w_router, 'b_router': b_router,
            'w1': w1, 'w3': w3, 'w2': w2, 'final_norm': final_norm}


def reference(x, mem, a_norm, a_w_in, a_conv, a_log, a_dt_bias, a_out_gain, a_w_out,
              kv_norm, w_kv, b_norm, b_w_in, b_w_out, mem_norm, w_mem_kv, ffn_norm,
              w_group, b_group, w_router, b_router, w1, w3, w2, final_norm):
    bsz, s, _ = x.shape
    m_len = mem.shape[1]
    n_a = DEPTH // 2
    h = x
    k_sh = None
    v_sh = None
    for l in range(DEPTH):
        mkv = rmsnorm(mem, mem_norm[l]) @ w_mem_kv[l]
        mem_k = mkv[..., :MEM_WIDTH].reshape(bsz, m_len, MEM_HEADS, MEM_DH)
        mem_v = mkv[..., MEM_WIDTH:].reshape(bsz, m_len, MEM_HEADS, MEM_DH)
        if l < n_a:
            h = h + mixer_a(h, a_norm[l], a_w_in[l], a_conv[l], a_log[l], a_dt_bias[l],
                            a_out_gain[l], a_w_out[l], mem_k, mem_v)
        else:
            if l == n_a:
                kv = rmsnorm(h, kv_norm) @ w_kv
                k_sh = kv[..., :SB_WIDTH].reshape(bsz, s, SB_HEADS, SB_DH).transpose(0, 2, 1, 3)
                v_sh = kv[..., SB_WIDTH:].reshape(bsz, s, SB_HEADS, SB_DH).transpose(0, 2, 1, 3)
            lb = l - n_a
            h = h + mixer_b(h, b_norm[lb], b_w_in[lb], b_w_out[lb], k_sh, v_sh, mem_k, mem_v)
        h = h + hier_moe(rmsnorm(h, ffn_norm[l]), w_group[l], b_group[l], w_router[l], b_router[l],
                         w1[l], w3[l], w2[l])
    return rmsnorm(h, final_norm)
```

```python
import functools

import jax
import jax.numpy as jnp
from jax import lax
from jax.experimental import pallas as pl
from jax.experimental.pallas import tpu as pltpu

EPS = 1e-6
GDN_HEADS = 6
GDN_D = 128
GDN_WIDTH = GDN_HEADS * GDN_D
CONV_K = 4
CHUNK = 64
SB_HEADS = 12
SB_DH = 64
SB_WIDTH = SB_HEADS * SB_DH
SB_BLOCK = 128
MEM_HEADS = 4
MEM_DH = 64
MEM_WIDTH = MEM_HEADS * MEM_DH
N_GROUPS = 4
EXPERTS_PER_GROUP = 4
N_EXPERTS = N_GROUPS * EXPERTS_PER_GROUP
EXPERT_FF = 256
LANES = 128
A_IN_PAD = 4 * GDN_WIDTH + MEM_WIDTH + LANES
ROUTER_LANE0 = N_GROUPS
VMEM_LIMIT = 48 * 1024 * 1024

F32 = jnp.float32
BF16 = jnp.bfloat16


def _dot(a, b):
    return jnp.dot(a, b, preferred_element_type=F32)


def _dot_nt(a, b):
    return lax.dot_general(a, b, (((1,), (1,)), ((), ())), preferred_element_type=F32)


def _dot_tn(a, b):
    return lax.dot_general(a, b, (((0,), (0,)), ((), ())), preferred_element_type=F32)


def _dot_f32(a, b):
    return jnp.dot(a, b, preferred_element_type=F32, precision=lax.Precision.HIGHEST)


def _sigmoid(x):
    return 1.0 / (1.0 + jnp.exp(-x))


def _softplus(x):
    return jnp.maximum(x, 0.0) + jnp.log(1.0 + jnp.exp(-jnp.abs(x)))


def _rms_scale(x):
    return lax.rsqrt(jnp.mean(x * x, axis=-1, keepdims=True) + EPS)


def _norm_matmul_kernel(x_ref, g_ref, w_ref, o_ref, *, col_chunk):
    x = x_ref[...]
    xn = (x * _rms_scale(x) * g_ref[...]).astype(BF16)
    n = o_ref.shape[-1]
    for c0 in range(0, n, col_chunk):
        c1 = min(n, c0 + col_chunk)
        o_ref[:, c0:c1] = _dot(xn, w_ref[:, c0:c1]).astype(o_ref.dtype)


def _norm_matmul(x2d, g, w, out_dtype, tm):
    t, d = x2d.shape
    n = w.shape[1]
    tm = min(tm, t)
    return pl.pallas_call(
        functools.partial(_norm_matmul_kernel, col_chunk=4 * LANES),
        grid=(t // tm,),
        in_specs=[pl.BlockSpec((tm, d), lambda i: (i, 0)),
                  pl.BlockSpec((1, d), lambda i: (0, 0)),
                  pl.BlockSpec((d, n), lambda i: (0, 0))],
        out_specs=pl.BlockSpec((tm, n), lambda i: (i, 0)),
        out_shape=jax.ShapeDtypeStruct((t, n), out_dtype),
        compiler_params=pltpu.CompilerParams(dimension_semantics=("parallel",), vmem_limit_bytes=VMEM_LIMIT),
        name="norm_matmul",
    )(x2d, g.reshape(1, d), w.astype(BF16))


def _gdn_kernel(qkv_ref, gate_ref, ba_ref, convw_ref, alog_ref, dtb_ref, ogain_ref, o_ref, xbuf, state):
    c = CHUNK
    tail = 8

    @pl.when(pl.program_id(1) == 0)
    def _():
        xbuf[0:tail, :] = jnp.zeros((tail, xbuf.shape[1]), F32)
        state[...] = jnp.zeros(state.shape, F32)

    xbuf[tail:tail + c, :] = qkv_ref[...]
    w = convw_ref[...]
    acc = xbuf[tail - 3:tail - 3 + c, :] * w[0:1, :]
    for j in range(1, CONV_K):
        acc = acc + xbuf[tail - 3 + j:tail - 3 + j + c, :] * w[j:j + 1, :]
    xbuf[0:tail, :] = xbuf[c:c + tail, :]
    x = acc * _sigmoid(acc)

    ba = ba_ref[...]
    beta_all = _sigmoid(ba)
    g_all = -jnp.exp(alog_ref[...]) * _softplus(ba + dtb_ref[...])
    row = lax.broadcasted_iota(jnp.int32, (c, c), 0)
    col = lax.broadcasted_iota(jnp.int32, (c, c), 1)
    incl = row >= col
    strict = row > col
    gc_all = _dot_f32(incl.astype(F32), g_all)
    gct_all = _dot_f32(g_all.T, (row <= col).astype(F32))
    eye = (row == col).astype(F32)
    ogain = ogain_ref[...]

    for h in range(GDN_HEADS):
        lo = h * GDN_D
        qh = x[:, lo:lo + GDN_D]
        kh = x[:, GDN_WIDTH + lo:GDN_WIDTH + lo + GDN_D]
        vh = x[:, 2 * GDN_WIDTH + lo:2 * GDN_WIDTH + lo + GDN_D]
        qh = qh * lax.rsqrt(jnp.sum(qh * qh, axis=-1, keepdims=True) + EPS) * (GDN_D ** -0.5)
        kh = kh * lax.rsqrt(jnp.sum(kh * kh, axis=-1, keepdims=True) + EPS)
        beta = beta_all[:, h:h + 1]
        gcol = gc_all[:, GDN_HEADS + h:GDN_HEADS + h + 1]
        grow = gct_all[GDN_HEADS + h:GDN_HEADS + h + 1, :]
        glast = gc_all[c - 1:c, GDN_HEADS + h:GDN_HEADS + h + 1]
        decay = jnp.where(incl, jnp.exp(jnp.where(incl, gcol - grow, 0.0)), 0.0)
        kb = kh * beta
        khb = kh.astype(BF16)
        a_mat = jnp.where(strict, _dot_nt(kb.astype(BF16), khb) * decay, 0.0)
        p = -a_mat
        tinv = eye + p
        for _ in range(5):
            pb = p.astype(BF16)
            p = _dot(pb, pb)
            tinv = tinv + _dot(tinv.astype(BF16), p.astype(BF16))
        egc = jnp.exp(gcol)
        rhs = jnp.concatenate([vh * beta, kb * egc], axis=1).astype(BF16)
        sol = _dot(tinv.astype(BF16), rhs)
        u = sol[:, :GDN_D]
        wmat = sol[:, GDN_D:]
        att = jnp.where(incl, _dot_nt(qh.astype(BF16), khb) * decay, 0.0)
        ktail = kh * jnp.exp(glast - gcol)
        s_old = state[h]
        sb = s_old.astype(BF16)
        v_new = u - _dot(wmat.astype(BF16), sb)
        vnb = v_new.astype(BF16)
        o = _dot((qh * egc).astype(BF16), sb) + _dot(att.astype(BF16), vnb)
        state[h] = s_old * jnp.exp(glast) + _dot_tn(ktail.astype(BF16), vnb)
        gate = gate_ref[:, lo:lo + GDN_D]
        o = o * _rms_scale(o) * ogain * (gate * _sigmoid(gate))
        o_ref[:, lo:lo + GDN_D] = o.astype(o_ref.dtype)


def _gdn(proj, conv_w, a_log, dt_bias, o_gain, bsz, seq):
    t = proj.shape[0]
    n_s = seq // CHUNK
    qkv_w = 3 * GDN_WIDTH
    pad = jnp.zeros((GDN_HEADS,), F32)
    alog_row = jnp.concatenate([pad, a_log.astype(F32), jnp.zeros((LANES - 2 * GDN_HEADS,), F32)]).reshape(1, LANES)
    dtb_row = jnp.concatenate([pad, dt_bias.astype(F32), jnp.zeros((LANES - 2 * GDN_HEADS,), F32)]).reshape(1, LANES)
    return pl.pallas_call(
        _gdn_kernel,
        grid=(bsz, n_s),
        in_specs=[pl.BlockSpec((CHUNK, qkv_w), lambda b, s: (b * n_s + s, 0)),
                  pl.BlockSpec((CHUNK, GDN_WIDTH), lambda b, s: (b * n_s + s, qkv_w // GDN_WIDTH)),
                  pl.BlockSpec((CHUNK, LANES), lambda b, s: (b * n_s + s, (A_IN_PAD - LANES) // LANES)),
                  pl.BlockSpec((CONV_K, qkv_w), lambda b, s: (0, 0)),
                  pl.BlockSpec((1, LANES), lambda b, s: (0, 0)),
                  pl.BlockSpec((1, LANES), lambda b, s: (0, 0)),
                  pl.BlockSpec((1, GDN_D), lambda b, s: (0, 0))],
        out_specs=pl.BlockSpec((CHUNK, GDN_WIDTH), lambda b, s: (b * n_s + s, 0)),
        out_shape=jax.ShapeDtypeStruct((t, GDN_WIDTH), BF16),
        scratch_shapes=[pltpu.VMEM((CHUNK + 8, qkv_w), F32),
                        pltpu.VMEM((GDN_HEADS, GDN_D, GDN_D), F32)],
        compiler_params=pltpu.CompilerParams(dimension_semantics=("parallel", "arbitrary"),
                                             vmem_limit_bytes=VMEM_LIMIT),
        name="gdn",
    )(proj, proj, proj, conv_w.astype(F32), alog_row, dtb_row, o_gain.astype(F32).reshape(1, GDN_D))


def _mix_out_kernel(h_ref, o_ref, mq_ref, mk_ref, mv_ref, wo_ref, wm_ref, out_ref):
    mq = (mq_ref[...] * (MEM_DH ** -0.5)).astype(BF16)
    mk = mk_ref[...]
    mv = mv_ref[...]
    head = lax.broadcasted_iota(jnp.int32, mk.shape, 1) // MEM_DH
    m = jnp.zeros(mq.shape, F32)
    for hh in range(MEM_HEADS):
        sc = _dot_nt(mq, jnp.where(head == hh, mk, 0.0).astype(BF16))
        sc = sc - jnp.max(sc, axis=-1, keepdims=True)
        p = jnp.exp(sc)
        p = p / jnp.sum(p, axis=-1, keepdims=True)
        m = m + _dot(p.astype(BF16), jnp.where(head == hh, mv, 0.0).astype(BF16))
    y = _dot(o_ref[...].astype(BF16), wo_ref[...]) + _dot(m.astype(BF16), wm_ref[...])
    out_ref[...] = h_ref[...] + y


def _mix_out(h2d, o, proj, mq_block, mkv, w_out, bsz, seq, ts):
    t, d = h2d.shape
    ts = min(ts, seq)
    n_s = seq // ts
    m_len = mkv.shape[0] // bsz
    ow = o.shape[1]
    return pl.pallas_call(
        _mix_out_kernel,
        grid=(bsz, n_s),
        in_specs=[pl.BlockSpec((ts, d), lambda b, s: (b * n_s + s, 0)),
                  pl.BlockSpec((ts, ow), lambda b, s: (b * n_s + s, 0)),
                  pl.BlockSpec((ts, MEM_WIDTH), lambda b, s: (b * n_s + s, mq_block)),
                  pl.BlockSpec((m_len, MEM_WIDTH), lambda b, s: (b, 0)),
                  pl.BlockSpec((m_len, MEM_WIDTH), lambda b, s: (b, 1)),
                  pl.BlockSpec((ow, d), lambda b, s: (0, 0)),
                  pl.BlockSpec((MEM_WIDTH, d), lambda b, s: (0, 0))],
        out_specs=pl.BlockSpec((ts, d), lambda b, s: (b * n_s + s, 0)),
        out_shape=jax.ShapeDtypeStruct((t, d), F32),
        compiler_params=pltpu.CompilerParams(dimension_semantics=("parallel", "parallel"),
                                             vmem_limit_bytes=VMEM_LIMIT),
        name="mix_out",
    )(h2d, o, proj, mkv, mkv, w_out[:ow].astype(BF16), w_out[ow:].astype(BF16))


def _sb_kernel(q_ref, k_ref, v_ref, o_ref):
    blk = SB_BLOCK
    i = pl.program_id(2)
    q = q_ref[...] * (SB_DH ** -0.5)
    lane = lax.broadcasted_iota(jnp.int32, q.shape, 1)
    q_heads = (jnp.where(lane < SB_DH, q, 0.0).astype(BF16), jnp.where(lane >= SB_DH, q, 0.0).astype(BF16))
    row = lax.broadcasted_iota(jnp.int32, (blk, blk), 0)
    col = lax.broadcasted_iota(jnp.int32, (blk, blk), 1)
    causal = col < row
    suffix = (row >= col).astype(BF16)

    def head_block(qh, kblk, vblk, carry, acc, diag):
        z = _dot_nt(qh, kblk)
        sp = _softplus(z)
        lg = -sp
        if diag:
            lg = jnp.where(causal, lg, 0.0)
        lg_hi = lg.astype(BF16)
        lg_lo = (lg - lg_hi.astype(F32)).astype(BF16)
        incl_sum = _dot(lg_hi, suffix) + _dot(lg_lo, suffix)
        a = jnp.exp(z - sp + (carry + incl_sum - lg))
        if diag:
            a = jnp.where(causal, a, 0.0)
        return carry + incl_sum[:, 0:1], acc + _dot(a.astype(BF16), vblk)

    def block(kb, state, diag):
        start = pl.multiple_of(kb * blk, blk)
        kblk = k_ref[pl.ds(start, blk), :]
        vblk = v_ref[pl.ds(start, blk), :]
        c0, c1, a0, a1 = state
        c0, a0 = head_block(q_heads[0], kblk, vblk, c0, a0, diag)
        c1, a1 = head_block(q_heads[1], kblk, vblk, c1, a1, diag)
        return c0, c1, a0, a1

    zc = jnp.zeros((blk, 1), F32)
    za = jnp.zeros((blk, 2 * SB_DH), F32)
    state = block(i, (zc, zc, za, za), True)
    state = lax.fori_loop(0, i, lambda j, st: block(i - 1 - j, st, False), state)
    o_ref[...] = jnp.where(lane < SB_DH, state[2], state[3]).astype(o_ref.dtype)


def _stick_breaking(qproj, kv, bsz, seq):
    t = qproj.shape[0]
    n_q = seq // SB_BLOCK
    pairs = SB_HEADS // 2
    pw = 2 * SB_DH
    return pl.pallas_call(
        _sb_kernel,
        grid=(bsz, pairs, n_q),
        in_specs=[pl.BlockSpec((SB_BLOCK, pw), lambda b, p, i: (b * n_q + i, p)),
                  pl.BlockSpec((seq, pw), lambda b, p, i: (b, p)),
                  pl.BlockSpec((seq, pw), lambda b, p, i: (b, pairs + p))],
        out_specs=pl.BlockSpec((SB_BLOCK, pw), lambda b, p, i: (b * n_q + i, p)),
        out_shape=jax.ShapeDtypeStruct((t, SB_WIDTH), BF16),
        compiler_params=pltpu.CompilerParams(dimension_semantics=("parallel", "parallel", "arbitrary"),
                                             vmem_limit_bytes=VMEM_LIMIT),
        name="stick_breaking",
    )(qproj, kv, kv)


def _moe_kernel(h_ref, g_ref, wr_ref, br_ref, w13_ref, w2_ref, fg_ref, out_ref, t_sc, comb_sc, acc_sc, *, final_norm):
    e = pl.program_id(1)
    tm = h_ref.shape[0]
    lane = lax.broadcasted_iota(jnp.int32, (tm, LANES), 1)

    @pl.when(e == 0)
    def _():
        x = h_ref[...]
        xn = x * _rms_scale(x) * g_ref[...]
        t_sc[...] = xn.astype(BF16)
        logits = _dot_f32(xn, wr_ref[...]) + br_ref[...]
        lane_f = lane.astype(F32)
        big = float(LANES)
        is_g = lane < N_GROUPS
        gl = jnp.where(is_g, logits, -jnp.inf)
        gmax = jnp.max(gl, axis=-1, keepdims=True)
        p_group = 1.0 / jnp.sum(jnp.where(is_g, jnp.exp(logits - gmax), 0.0), axis=-1, keepdims=True)
        gsel = jnp.min(jnp.where(gl == gmax, lane_f, big), axis=-1, keepdims=True)
        e_lane = lane - ROUTER_LANE0
        grp_of_lane = (e_lane // EXPERTS_PER_GROUP).astype(F32)
        in_grp = jnp.logical_and(jnp.logical_and(e_lane >= 0, e_lane < N_EXPERTS), grp_of_lane == gsel)
        el = jnp.where(in_grp, logits, -jnp.inf)
        emax = jnp.max(el, axis=-1, keepdims=True)
        ee = jnp.where(in_grp, jnp.exp(logits - emax), 0.0)
        prob = ee / jnp.sum(ee, axis=-1, keepdims=True)
        pm = jnp.where(in_grp, prob, -1.0)
        p1 = jnp.max(pm, axis=-1, keepdims=True)
        i1 = jnp.min(jnp.where(pm == p1, lane_f, big), axis=-1, keepdims=True)
        pm2 = jnp.where(lane_f == i1, -1.0, pm)
        p2 = jnp.max(pm2, axis=-1, keepdims=True)
        i2 = jnp.min(jnp.where(pm2 == p2, lane_f, big), axis=-1, keepdims=True)
        sel = jnp.logical_or(lane_f == i1, lane_f == i2)
        comb_sc[...] = jnp.where(sel, p_group * (prob / (p1 + p2)), 0.0)
        acc_sc[...] = jnp.zeros(acc_sc.shape, F32)

    hcat = _dot(t_sc[...], w13_ref[...])
    hg = hcat[:, :EXPERT_FF]
    hid = hg * _sigmoid(hg) * hcat[:, EXPERT_FF:]
    c_e = jnp.sum(jnp.where(lane == e + ROUTER_LANE0, comb_sc[...], 0.0), axis=-1, keepdims=True)
    acc_sc[...] += _dot((hid * c_e).astype(BF16), w2_ref[...])

    @pl.when(e == N_EXPERTS - 1)
    def _():
        y = h_ref[...] + acc_sc[...]
        if final_norm:
            y = y * _rms_scale(y) * fg_ref[...]
        out_ref[...] = y


def _moe(h2d, ffn_g, w_group, b_group, w_router, b_router, w1, w3, w2, final_g, final_norm, tm):
    t, d = h2d.shape
    tm = min(tm, t)
    pad = LANES - N_GROUPS - N_EXPERTS
    wr = jnp.concatenate([w_group, w_router, jnp.zeros((d, pad), F32)], axis=1).astype(F32)
    br = jnp.concatenate([b_group, b_router, jnp.zeros((pad,), F32)]).astype(F32).reshape(1, LANES)
    w13 = jnp.concatenate([w1, w3], axis=-1).astype(BF16)
    return pl.pallas_call(
        functools.partial(_moe_kernel, final_norm=final_norm),
        grid=(t // tm, N_EXPERTS),
        in_specs=[pl.BlockSpec((tm, d), lambda i, e: (i, 0)),
                  pl.BlockSpec((1, d), lambda i, e: (0, 0)),
                  pl.BlockSpec((d, LANES), lambda i, e: (0, 0)),
                  pl.BlockSpec((1, LANES), lambda i, e: (0, 0)),
                  pl.BlockSpec((None, d, 2 * EXPERT_FF), lambda i, e: (e, 0, 0)),
                  pl.BlockSpec((None, EXPERT_FF, d), lambda i, e: (e, 0, 0)),
                  pl.BlockSpec((1, d), lambda i, e: (0, 0))],
        out_specs=pl.BlockSpec((tm, d), lambda i, e: (i, 0)),
        out_shape=jax.ShapeDtypeStruct((t, d), F32),
        scratch_shapes=[pltpu.VMEM((tm, d), BF16), pltpu.VMEM((tm, LANES), F32), pltpu.VMEM((tm, d), F32)],
        compiler_params=pltpu.CompilerParams(dimension_semantics=("parallel", "arbitrary"),
                                             vmem_limit_bytes=VMEM_LIMIT),
        name="moe",
    )(h2d, ffn_g.reshape(1, d), wr, br, w13, w2.astype(BF16), final_g.reshape(1, d))


def kernel(x, mem, a_norm, a_w_in, a_conv, a_log, a_dt_bias, a_out_gain, a_w_out, kv_norm, w_kv, b_norm, b_w_in,
           b_w_out, mem_norm, w_mem_kv, ffn_norm, w_group, b_group, w_router, b_router, w1, w3, w2, final_norm):
    bsz, seq, d = x.shape
    m_len = mem.shape[1]
    depth = mem_norm.shape[0]
    n_a = a_norm.shape[0]
    h = x.reshape(bsz * seq, d)
    mem2d = mem.reshape(bsz * m_len, d)
    kv = None
    for l in range(depth):
        mkv = _norm_matmul(mem2d, mem_norm[l], w_mem_kv[l], F32, 256)
        if l < n_a:
            w_in = a_w_in[l]
            g4 = 4 * GDN_WIDTH
            w_perm = jnp.concatenate([w_in[:, :g4], w_in[:, g4 + 2 * GDN_HEADS:], w_in[:, g4:g4 + 2 * GDN_HEADS],
                                      jnp.zeros((d, LANES - 2 * GDN_HEADS), w_in.dtype)], axis=1)
            proj = _norm_matmul(h, a_norm[l], w_perm, F32, 256)
            o = _gdn(proj, a_conv[l], a_log[l], a_dt_bias[l], a_out_gain[l], bsz, seq)
            h = _mix_out(h, o, proj, g4 // MEM_WIDTH, mkv, a_w_out[l], bsz, seq, 256)
        else:
            lb = l - n_a
            if l == n_a:
                kv = _norm_matmul(h, kv_norm, w_kv, BF16, 256)
            proj = _norm_matmul(h, b_norm[lb], b_w_in[lb], F32, 256)
            o = _stick_breaking(proj, kv, bsz, seq)
            h = _mix_out(h, o, proj, SB_WIDTH // MEM_WIDTH, mkv, b_w_out[lb], bsz, seq, 256)
        h = _moe(h, ffn_norm[l], w_group[l], b_group[l], w_router[l], b_router[l], w1[l], w3[l], w2[l],
                 final_norm, l == depth - 1, 512)
    return h.reshape(bsz, seq, d)
```

```python
import functools

import jax
import jax.numpy as jnp
from jax import lax
from jax.experimental import pallas as pl
from jax.experimental.pallas import tpu as pltpu

EPS = 1e-6
GDN_HEADS = 6
GDN_D = 128
GDN_WIDTH = GDN_HEADS * GDN_D
CONV_K = 4
CHUNK = 64
SB_HEADS = 12
SB_DH = 64
SB_WIDTH = SB_HEADS * SB_DH
SB_BLOCK = 128
MEM_HEADS = 4
MEM_DH = 64
MEM_WIDTH = MEM_HEADS * MEM_DH
N_GROUPS = 4
EXPERTS_PER_GROUP = 4
N_EXPERTS = N_GROUPS * EXPERTS_PER_GROUP
EXPERT_FF = 256
LANES = 128
A_IN_PAD = 4 * GDN_WIDTH + MEM_WIDTH + LANES
ROUTER_LANE0 = N_GROUPS
VMEM_LIMIT = 48 * 1024 * 1024

F32 = jnp.float32
BF16 = jnp.bfloat16


def _dot(a, b):
    return jnp.dot(a, b, preferred_element_type=F32)


def _dot_nt(a, b):
    return lax.dot_general(a, b, (((1,), (1,)), ((), ())), preferred_element_type=F32)


def _dot_tn(a, b):
    return lax.dot_general(a, b, (((0,), (0,)), ((), ())), preferred_element_type=F32)


def _dot_f32(a, b):
    return jnp.dot(a, b, preferred_element_type=F32, precision=lax.Precision.HIGHEST)


def _sigmoid(x):
    return 1.0 / (1.0 + jnp.exp(-x))


def _softplus(x):
    return jnp.maximum(x, 0.0) + jnp.log(1.0 + jnp.exp(-jnp.abs(x)))


def _rms_scale(x):
    return lax.rsqrt(jnp.mean(x * x, axis=-1, keepdims=True) + EPS)


def _norm_matmul_kernel(x_ref, g_ref, w_ref, o_ref, *, col_chunk):
    x = x_ref[...]
    xn = (x * _rms_scale(x) * g_ref[...]).astype(BF16)
    n = o_ref.shape[-1]
    for c0 in range(0, n, col_chunk):
        c1 = min(n, c0 + col_chunk)
        o_ref[:, c0:c1] = _dot(xn, w_ref[:, c0:c1]).astype(o_ref.dtype)


def _norm_matmul(x2d, g, w, out_dtype, tm):
    t, d = x2d.shape
    n = w.shape[1]
    tm = min(tm, t)
    return pl.pallas_call(
        functools.partial(_norm_matmul_kernel, col_chunk=4 * LANES),
        grid=(t // tm,),
        in_specs=[pl.BlockSpec((tm, d), lambda i: (i, 0)),
                  pl.BlockSpec((1, d), lambda i: (0, 0)),
                  pl.BlockSpec((d, n), lambda i: (0, 0))],
        out_specs=pl.BlockSpec((tm, n), lambda i: (i, 0)),
        out_shape=jax.ShapeDtypeStruct((t, n), out_dtype),
        compiler_params=pltpu.CompilerParams(dimension_semantics=("parallel",), vmem_limit_bytes=VMEM_LIMIT),
        name="norm_matmul",
    )(x2d, g.reshape(1, d), w.astype(BF16))


GDN_STEP_CHUNKS = 2
GDN_TOK = CHUNK * GDN_STEP_CHUNKS
CONV_TAIL = 8


def _gdn_kernel(qkv_ref, gate_ref, ba_ref, convw_ref, alog_ref, dtb_ref, ogain_ref, o_ref, xbuf, state):
    c = CHUNK
    tok = GDN_TOK

    @pl.when(pl.program_id(1) == 0)
    def _():
        xbuf[0:CONV_TAIL, :] = jnp.zeros((CONV_TAIL, xbuf.shape[1]), F32)
        state[...] = jnp.zeros(state.shape, F32)

    xbuf[CONV_TAIL:CONV_TAIL + tok, :] = qkv_ref[...]

    def conv_act(lo):
        base = CONV_TAIL - (CONV_K - 1)
        acc = xbuf[base:base + tok, lo:lo + GDN_D] * convw_ref[0:1, lo:lo + GDN_D]
        for j in range(1, CONV_K):
            acc = acc + xbuf[base + j:base + j + tok, lo:lo + GDN_D] * convw_ref[j:j + 1, lo:lo + GDN_D]
        return acc * _sigmoid(acc)

    ba = ba_ref[...]
    beta_all = _sigmoid(ba)
    g_all = -jnp.exp(alog_ref[...]) * _softplus(ba + dtb_ref[...])
    trow = lax.broadcasted_iota(jnp.int32, (tok, tok), 0)
    tcol = lax.broadcasted_iota(jnp.int32, (tok, tok), 1)
    same_chunk = (trow // c) == (tcol // c)
    lower = jnp.where(jnp.logical_and(same_chunk, trow >= tcol), 1.0, 0.0)
    upper = jnp.where(jnp.logical_and(same_chunk, trow <= tcol), 1.0, 0.0)
    gc_all = _dot_f32(lower, g_all)
    gct_all = _dot_f32(g_all.T, upper)
    row = lax.broadcasted_iota(jnp.int32, (c, c), 0)
    col = lax.broadcasted_iota(jnp.int32, (c, c), 1)
    incl = row >= col
    strict = row > col
    eye = (row == col).astype(F32)
    ogain = ogain_ref[...]

    heads = range(GDN_HEADS)
    chunks = range(GDN_STEP_CHUNKS)
    items = [(ci, h) for ci in chunks for h in heads]

    q_full, k_full, v_full = [], [], []
    for h in heads:
        lo = h * GDN_D
        qf = conv_act(lo)
        kf = conv_act(GDN_WIDTH + lo)
        q_full.append(qf * lax.rsqrt(jnp.sum(qf * qf, axis=-1, keepdims=True) + EPS) * (GDN_D ** -0.5))
        k_full.append(kf * lax.rsqrt(jnp.sum(kf * kf, axis=-1, keepdims=True) + EPS))
        v_full.append(conv_act(2 * GDN_WIDTH + lo))

    pre = {}
    for ci, h in items:
        r0 = ci * c
        gl = GDN_HEADS + h
        qh = q_full[h][r0:r0 + c]
        kh = k_full[h][r0:r0 + c]
        vh = v_full[h][r0:r0 + c]
        beta = beta_all[r0:r0 + c, h:h + 1]
        gcol = gc_all[r0:r0 + c, gl:gl + 1]
        grow = gct_all[gl:gl + 1, r0:r0 + c]
        glast = gc_all[r0 + c - 1:r0 + c, gl:gl + 1]
        decay = jnp.where(incl, jnp.exp(jnp.where(incl, gcol - grow, 0.0)), 0.0)
        kb = kh * beta
        khb = kh.astype(BF16)
        egc = jnp.exp(gcol)
        pre[ci, h] = dict(
            p=-jnp.where(strict, _dot_nt(kb.astype(BF16), khb) * decay, 0.0),
            att=jnp.where(incl, _dot_nt(qh.astype(BF16), khb) * decay, 0.0).astype(BF16),
            rhs=jnp.concatenate([vh * beta, kb * egc], axis=1).astype(BF16),
            qg=(qh * egc).astype(BF16),
            ktail=(kh * jnp.exp(glast - gcol)).astype(BF16),
            sdecay=jnp.exp(glast))

    tinv = {it: eye + pre[it]["p"] for it in items}
    pw = {it: pre[it]["p"] for it in items}
    for _ in range(5):
        for it in items:
            pb = pw[it].astype(BF16)
            pw[it] = _dot(pb, pb)
        for it in items:
            tinv[it] = tinv[it] + _dot(tinv[it].astype(BF16), pw[it].astype(BF16))
    sol = {it: _dot(tinv[it].astype(BF16), pre[it]["rhs"]) for it in items}

    s_cur = [state[h] for h in heads]
    for ci in chunks:
        r0 = ci * c
        for h in heads:
            pr = pre[ci, h]
            lo = h * GDN_D
            sb = s_cur[h].astype(BF16)
            v_new = sol[ci, h][:, :GDN_D] - _dot(sol[ci, h][:, GDN_D:].astype(BF16), sb)
            vnb = v_new.astype(BF16)
            o = _dot(pr["qg"], sb) + _dot(pr["att"], vnb)
            s_cur[h] = s_cur[h] * pr["sdecay"] + _dot_tn(pr["ktail"], vnb)
            gate = gate_ref[r0:r0 + c, lo:lo + GDN_D]
            o = o * _rms_scale(o) * ogain * (gate * _sigmoid(gate))
            o_ref[r0:r0 + c, lo:lo + GDN_D] = o.astype(o_ref.dtype)
    for h in heads:
        state[h] = s_cur[h]
    xbuf[0:CONV_TAIL, :] = xbuf[tok:tok + CONV_TAIL, :]


def _gdn(proj, conv_w, a_log, dt_bias, o_gain, bsz, seq):
    t = proj.shape[0]
    n_s = seq // GDN_TOK
    qkv_w = 3 * GDN_WIDTH
    pad = jnp.zeros((GDN_HEADS,), F32)
    alog_row = jnp.concatenate([pad, a_log.astype(F32), jnp.zeros((LANES - 2 * GDN_HEADS,), F32)]).reshape(1, LANES)
    dtb_row = jnp.concatenate([pad, dt_bias.astype(F32), jnp.zeros((LANES - 2 * GDN_HEADS,), F32)]).reshape(1, LANES)
    return pl.pallas_call(
        _gdn_kernel,
        grid=(bsz, n_s),
        in_specs=[pl.BlockSpec((GDN_TOK, qkv_w), lambda b, s: (b * n_s + s, 0)),
                  pl.BlockSpec((GDN_TOK, GDN_WIDTH), lambda b, s: (b * n_s + s, qkv_w // GDN_WIDTH)),
                  pl.BlockSpec((GDN_TOK, LANES), lambda b, s: (b * n_s + s, (A_IN_PAD - LANES) // LANES)),
                  pl.BlockSpec((CONV_K, qkv_w), lambda b, s: (0, 0)),
                  pl.BlockSpec((1, LANES), lambda b, s: (0, 0)),
                  pl.BlockSpec((1, LANES), lambda b, s: (0, 0)),
                  pl.BlockSpec((1, GDN_D), lambda b, s: (0, 0))],
        out_specs=pl.BlockSpec((GDN_TOK, GDN_WIDTH), lambda b, s: (b * n_s + s, 0)),
        out_shape=jax.ShapeDtypeStruct((t, GDN_WIDTH), BF16),
        scratch_shapes=[pltpu.VMEM((GDN_TOK + CONV_TAIL, qkv_w), F32),
                        pltpu.VMEM((GDN_HEADS, GDN_D, GDN_D), F32)],
        compiler_params=pltpu.CompilerParams(dimension_semantics=("parallel", "arbitrary"),
                                             vmem_limit_bytes=VMEM_LIMIT),
        name="gdn",
    )(proj, proj, proj, conv_w.astype(F32), alog_row, dtb_row, o_gain.astype(F32).reshape(1, GDN_D))


def _mix_out_kernel(h_ref, o_ref, mq_ref, mk_ref, mv_ref, wo_ref, wm_ref, out_ref):
    mq = (mq_ref[...] * (MEM_DH ** -0.5)).astype(BF16)
    mk = mk_ref[...]
    mv = mv_ref[...]
    head = lax.broadcasted_iota(jnp.int32, mk.shape, 1) // MEM_DH
    m = jnp.zeros(mq.shape, F32)
    for hh in range(MEM_HEADS):
        sc = _dot_nt(mq, jnp.where(head == hh, mk, 0.0).astype(BF16))
        sc = sc - jnp.max(sc, axis=-1, keepdims=True)
        p = jnp.exp(sc)
        p = p / jnp.sum(p, axis=-1, keepdims=True)
        m = m + _dot(p.astype(BF16), jnp.where(head == hh, mv, 0.0).astype(BF16))
    y = _dot(o_ref[...].astype(BF16), wo_ref[...]) + _dot(m.astype(BF16), wm_ref[...])
    out_ref[...] = h_ref[...] + y


def _mix_out(h2d, o, proj, mq_block, mkv, w_out, bsz, seq, ts):
    t, d = h2d.shape
    ts = min(ts, seq)
    n_s = seq // ts
    m_len = mkv.shape[0] // bsz
    ow = o.shape[1]
    return pl.pallas_call(
        _mix_out_kernel,
        grid=(bsz, n_s),
        in_specs=[pl.BlockSpec((ts, d), lambda b, s: (b * n_s + s, 0)),
                  pl.BlockSpec((ts, ow), lambda b, s: (b * n_s + s, 0)),
                  pl.BlockSpec((ts, MEM_WIDTH), lambda b, s: (b * n_s + s, mq_block)),
                  pl.BlockSpec((m_len, MEM_WIDTH), lambda b, s: (b, 0)),
                  pl.BlockSpec((m_len, MEM_WIDTH), lambda b, s: (b, 1)),
                  pl.BlockSpec((ow, d), lambda b, s: (0, 0)),
                  pl.BlockSpec((MEM_WIDTH, d), lambda b, s: (0, 0))],
        out_specs=pl.BlockSpec((ts, d), lambda b, s: (b * n_s + s, 0)),
        out_shape=jax.ShapeDtypeStruct((t, d), F32),
        compiler_params=pltpu.CompilerParams(dimension_semantics=("parallel", "parallel"),
                                             vmem_limit_bytes=VMEM_LIMIT),
        name="mix_out",
    )(h2d, o, proj, mkv, mkv, w_out[:ow].astype(BF16), w_out[ow:].astype(BF16))


SB_SUB = 4
SB_NEG = -1e30
SB_DONE = 88.0


def _split_bf16(x):
    hi = pltpu.bitcast(pltpu.bitcast(x, jnp.uint32) & jnp.uint32(0xFFFF0000), F32)
    return hi.astype(BF16), (x - hi).astype(BF16)


def _sb_kernel(q_ref, k_ref, v_ref, o_ref):
    blk = SB_BLOCK
    step = pl.program_id(2)
    lane = lax.broadcasted_iota(jnp.int32, (blk, 2 * SB_DH), 1)
    row2 = lax.broadcasted_iota(jnp.int32, (blk, 2 * blk), 0)
    col2 = lax.broadcasted_iota(jnp.int32, (blk, 2 * blk), 1)
    causal2 = col2 < row2 + blk
    srow2 = lax.broadcasted_iota(jnp.int32, (2 * blk, 2 * blk), 0)
    scol2 = lax.broadcasted_iota(jnp.int32, (2 * blk, 2 * blk), 1)
    suffix2 = (srow2 >= scol2).astype(BF16)
    suffix1 = suffix2[:blk, :blk]

    def suffix_sums(sp, suffix):
        hi, lo = _split_bf16(sp)
        return _dot(hi, suffix) + _dot(lo, suffix)

    q_heads, k2s, v2s = [], [], []
    for j in range(SB_SUB):
        i = step * SB_SUB + j
        q = q_ref[j * blk:(j + 1) * blk, :] * (SB_DH ** -0.5)
        q_heads.append(jnp.where(lane < SB_DH, q, 0.0).astype(BF16))
        q_heads.append(jnp.where(lane >= SB_DH, q, 0.0).astype(BF16))
        prev = pl.multiple_of(jnp.maximum(i - 1, 0) * blk, blk)
        diag = pl.multiple_of(i * blk, blk)
        k2s.append(jnp.concatenate([k_ref[pl.ds(prev, blk), :], k_ref[pl.ds(diag, blk), :]], axis=0))
        has_prev = jnp.where(i > 0, 1.0, 0.0).astype(BF16)
        v2s.append(jnp.concatenate([v_ref[pl.ds(prev, blk), :] * has_prev, v_ref[pl.ds(diag, blk), :]], axis=0))
    chains = range(2 * SB_SUB)
    zs = [jnp.where(causal2, _dot_nt(q_heads[n], k2s[n // 2]), SB_NEG) for n in chains]
    splits = [_split_bf16(_softplus(zs[n])) for n in chains]
    sums = [_dot(splits[n][0], suffix2) + _dot(splits[n][1], suffix2) for n in chains]
    probs = [jnp.exp(zs[n] - sums[n]).astype(BF16) for n in chains]
    carries = [sums[n][:, 0:1] for n in chains]
    accs = [_dot(probs[n], v2s[n // 2]) for n in chains]

    def lowest(cs):
        m = cs[0]
        for cc in cs[1:]:
            m = jnp.minimum(m, cc)
        return jnp.min(m)

    def more(st):
        it, cmin = st[0], st[1]
        return jnp.logical_and(it < (step + 1) * SB_SUB - 2, cmin < SB_DONE)

    def body(st):
        it = st[0]
        cs = list(st[2:2 + 2 * SB_SUB])
        acs = list(st[2 + 2 * SB_SUB:])
        kblks, vblks = [], []
        for j in range(SB_SUB):
            kb = step * SB_SUB + j - 2 - it
            start = pl.multiple_of(jnp.maximum(kb, 0) * blk, blk)
            kblks.append(k_ref[pl.ds(start, blk), :])
            vblks.append(v_ref[pl.ds(start, blk), :] * jnp.where(kb >= 0, 1.0, 0.0).astype(BF16))
        zl = [_dot_nt(q_heads[n], kblks[n // 2]) for n in chains]
        sl = [suffix_sums(_softplus(zl[n]), suffix1) for n in chains]
        pl_ = [jnp.exp(zl[n] - cs[n] - sl[n]).astype(BF16) for n in chains]
        acs = [acs[n] + _dot(pl_[n], vblks[n // 2]) for n in chains]
        cs = [cs[n] + sl[n][:, 0:1] for n in chains]
        return (it + 1, lowest(cs), *cs, *acs)

    st = lax.while_loop(more, body, (jnp.int32(0), lowest(carries), *carries, *accs))
    accs = st[2 + 2 * SB_SUB:]
    for j in range(SB_SUB):
        o_ref[j * blk:(j + 1) * blk, :] = jnp.where(lane < SB_DH, accs[2 * j], accs[2 * j + 1]).astype(o_ref.dtype)


def _stick_breaking(qproj, kv, bsz, seq):
    t = qproj.shape[0]
    tq = SB_SUB * SB_BLOCK
    n_q = seq // tq
    pairs = SB_HEADS // 2
    pw = 2 * SB_DH
    return pl.pallas_call(
        _sb_kernel,
        grid=(bsz, pairs, n_q),
        in_specs=[pl.BlockSpec((tq, pw), lambda b, p, i: (b * n_q + i, p)),
                  pl.BlockSpec((seq, pw), lambda b, p, i: (b, p)),
                  pl.BlockSpec((seq, pw), lambda b, p, i: (b, pairs + p))],
        out_specs=pl.BlockSpec((tq, pw), lambda b, p, i: (b * n_q + i, p)),
        out_shape=jax.ShapeDtypeStruct((t, SB_WIDTH), BF16),
        compiler_params=pltpu.CompilerParams(dimension_semantics=("parallel", "parallel", "arbitrary"),
                                             vmem_limit_bytes=VMEM_LIMIT),
        name="stick_breaking",
    )(qproj, kv, kv)


def _moe_kernel(h_ref, g_ref, wr_ref, br_ref, w13_ref, w2_ref, fg_ref, out_ref, t_sc, comb_sc, acc_sc, *, final_norm):
    e = pl.program_id(1)
    tm = h_ref.shape[0]
    lane = lax.broadcasted_iota(jnp.int32, (tm, LANES), 1)

    @pl.when(e == 0)
    def _():
        x = h_ref[...]
        xn = x * _rms_scale(x) * g_ref[...]
        t_sc[...] = xn.astype(BF16)
        logits = _dot_f32(xn, wr_ref[...]) + br_ref[...]
        lane_f = lane.astype(F32)
        big = float(LANES)
        is_g = lane < N_GROUPS
        gl = jnp.where(is_g, logits, -jnp.inf)
        gmax = jnp.max(gl, axis=-1, keepdims=True)
        p_group = 1.0 / jnp.sum(jnp.where(is_g, jnp.exp(logits - gmax), 0.0), axis=-1, keepdims=True)
        gsel = jnp.min(jnp.where(gl == gmax, lane_f, big), axis=-1, keepdims=True)
        e_lane = lane - ROUTER_LANE0
        grp_of_lane = (e_lane // EXPERTS_PER_GROUP).astype(F32)
        in_grp = jnp.logical_and(jnp.logical_and(e_lane >= 0, e_lane < N_EXPERTS), grp_of_lane == gsel)
        el = jnp.where(in_grp, logits, -jnp.inf)
        emax = jnp.max(el, axis=-1, keepdims=True)
        ee = jnp.where(in_grp, jnp.exp(logits - emax), 0.0)
        prob = ee / jnp.sum(ee, axis=-1, keepdims=True)
        pm = jnp.where(in_grp, prob, -1.0)
        p1 = jnp.max(pm, axis=-1, keepdims=True)
        i1 = jnp.min(jnp.where(pm == p1, lane_f, big), axis=-1, keepdims=True)
        pm2 = jnp.where(lane_f == i1, -1.0, pm)
        p2 = jnp.max(pm2, axis=-1, keepdims=True)
        i2 = jnp.min(jnp.where(pm2 == p2, lane_f, big), axis=-1, keepdims=True)
        sel = jnp.logical_or(lane_f == i1, lane_f == i2)
        comb_sc[...] = jnp.where(sel, p_group * (prob / (p1 + p2)), 0.0)
        acc_sc[...] = jnp.zeros(acc_sc.shape, F32)

    hcat = _dot(t_sc[...], w13_ref[...])
    hg = hcat[:, :EXPERT_FF]
    hid = hg * _sigmoid(hg) * hcat[:, EXPERT_FF:]
    c_e = jnp.sum(jnp.where(lane == e + ROUTER_LANE0, comb_sc[...], 0.0), axis=-1, keepdims=True)
    acc_sc[...] += _dot((hid * c_e).astype(BF16), w2_ref[...])

    @pl.when(e == N_EXPERTS - 1)
    def _():
        y = h_ref[...] + acc_sc[...]
        if final_norm:
            y = y * _rms_scale(y) * fg_ref[...]
        out_ref[...] = y


def _moe(h2d, ffn_g, w_group, b_group, w_router, b_router, w1, w3, w2, final_g, final_norm, tm):
    t, d = h2d.shape
    tm = min(tm, t)
    pad = LANES - N_GROUPS - N_EXPERTS
    wr = jnp.concatenate([w_group, w_router, jnp.zeros((d, pad), F32)], axis=1).astype(F32)
    br = jnp.concatenate([b_group, b_router, jnp.zeros((pad,), F32)]).astype(F32).reshape(1, LANES)
    w13 = jnp.concatenate([w1, w3], axis=-1).astype(BF16)
    return pl.pallas_call(
        functools.partial(_moe_kernel, final_norm=final_norm),
        grid=(t // tm, N_EXPERTS),
        in_specs=[pl.BlockSpec((tm, d), lambda i, e: (i, 0)),
                  pl.BlockSpec((1, d), lambda i, e: (0, 0)),
                  pl.BlockSpec((d, LANES), lambda i, e: (0, 0)),
                  pl.BlockSpec((1, LANES), lambda i, e: (0, 0)),
                  pl.BlockSpec((None, d, 2 * EXPERT_FF), lambda i, e: (e, 0, 0)),
                  pl.BlockSpec((None, EXPERT_FF, d), lambda i, e: (e, 0, 0)),
                  pl.BlockSpec((1, d), lambda i, e: (0, 0))],
        out_specs=pl.BlockSpec((tm, d), lambda i, e: (i, 0)),
        out_shape=jax.ShapeDtypeStruct((t, d), F32),
        scratch_shapes=[pltpu.VMEM((tm, d), BF16), pltpu.VMEM((tm, LANES), F32), pltpu.VMEM((tm, d), F32)],
        compiler_params=pltpu.CompilerParams(dimension_semantics=("parallel", "arbitrary"),
                                             vmem_limit_bytes=VMEM_LIMIT),
        name="moe",
    )(h2d, ffn_g.reshape(1, d), wr, br, w13, w2.astype(BF16), final_g.reshape(1, d))


def kernel(x, mem, a_norm, a_w_in, a_conv, a_log, a_dt_bias, a_out_gain, a_w_out, kv_norm, w_kv, b_norm, b_w_in,
           b_w_out, mem_norm, w_mem_kv, ffn_norm, w_group, b_group, w_router, b_router, w1, w3, w2, final_norm):
    bsz, seq, d = x.shape
    m_len = mem.shape[1]
    depth = mem_norm.shape[0]
    n_a = a_norm.shape[0]
    h = x.reshape(bsz * seq, d)
    mem2d = mem.reshape(bsz * m_len, d)
    kv = None
    for l in range(depth):
        mkv = _norm_matmul(mem2d, mem_norm[l], w_mem_kv[l], F32, 256)
        if l < n_a:
            w_in = a_w_in[l]
            g4 = 4 * GDN_WIDTH
            w_perm = jnp.concatenate([w_in[:, :g4], w_in[:, g4 + 2 * GDN_HEADS:], w_in[:, g4:g4 + 2 * GDN_HEADS],
                                      jnp.zeros((d, LANES - 2 * GDN_HEADS), w_in.dtype)], axis=1)
            proj = _norm_matmul(h, a_norm[l], w_perm, F32, 256)
            o = _gdn(proj, a_conv[l], a_log[l], a_dt_bias[l], a_out_gain[l], bsz, seq)
            h = _mix_out(h, o, proj, g4 // MEM_WIDTH, mkv, a_w_out[l], bsz, seq, 256)
        else:
            lb = l - n_a
            if l == n_a:
                kv = _norm_matmul(h, kv_norm, w_kv, BF16, 256)
            proj = _norm_matmul(h, b_norm[lb], b_w_in[lb], F32, 256)
            o = _stick_breaking(proj, kv, bsz, seq)
            h = _mix_out(h, o, proj, SB_WIDTH // MEM_WIDTH, mkv, b_w_out[lb], bsz, seq, 256)
        h = _moe(h, ffn_norm[l], w_group[l], b_group[l], w_router[l], b_router[l], w1[l], w3[l], w2[l],
                 final_norm, l == depth - 1, 512)
    return h.reshape(bsz, seq, d)
```

```python
import functools

import jax
import jax.numpy as jnp
from jax import lax
from jax.experimental import pallas as pl
from jax.experimental.pallas import tpu as pltpu

EPS = 1e-6
GDN_HEADS = 6
GDN_D = 128
GDN_WIDTH = GDN_HEADS * GDN_D
CONV_K = 4
CHUNK = 64
SB_HEADS = 12
SB_DH = 64
SB_WIDTH = SB_HEADS * SB_DH
SB_BLOCK = 128
MEM_HEADS = 4
MEM_DH = 64
MEM_WIDTH = MEM_HEADS * MEM_DH
N_GROUPS = 4
EXPERTS_PER_GROUP = 4
N_EXPERTS = N_GROUPS * EXPERTS_PER_GROUP
EXPERT_FF = 256
LANES = 128
A_IN_PAD = 4 * GDN_WIDTH + MEM_WIDTH + LANES
ROUTER_LANE0 = N_GROUPS
VMEM_LIMIT = 48 * 1024 * 1024

F32 = jnp.float32
BF16 = jnp.bfloat16


def _dot(a, b):
    return jnp.dot(a, b, preferred_element_type=F32)


def _dot_nt(a, b):
    return lax.dot_general(a, b, (((1,), (1,)), ((), ())), preferred_element_type=F32)


def _dot_tn(a, b):
    return lax.dot_general(a, b, (((0,), (0,)), ((), ())), preferred_element_type=F32)


def _dot_f32(a, b):
    return jnp.dot(a, b, preferred_element_type=F32, precision=lax.Precision.HIGHEST)


def _sigmoid(x):
    return 1.0 / (1.0 + jnp.exp(-x))


def _softplus(x):
    return jnp.maximum(x, 0.0) + jnp.log(1.0 + jnp.exp(-jnp.abs(x)))


def _rms_scale(x):
    return lax.rsqrt(jnp.mean(x * x, axis=-1, keepdims=True) + EPS)


def _norm_matmul_kernel(x_ref, g_ref, w_ref, o_ref, *, col_chunk):
    x = x_ref[...]
    xn = (x * _rms_scale(x) * g_ref[...]).astype(BF16)
    n = o_ref.shape[-1]
    for c0 in range(0, n, col_chunk):
        c1 = min(n, c0 + col_chunk)
        o_ref[:, c0:c1] = _dot(xn, w_ref[:, c0:c1]).astype(o_ref.dtype)


def _norm_matmul(x2d, g, w, out_dtype, tm):
    t, d = x2d.shape
    n = w.shape[1]
    tm = min(tm, t)
    return pl.pallas_call(
        functools.partial(_norm_matmul_kernel, col_chunk=4 * LANES),
        grid=(t // tm,),
        in_specs=[pl.BlockSpec((tm, d), lambda i: (i, 0)),
                  pl.BlockSpec((1, d), lambda i: (0, 0)),
                  pl.BlockSpec((d, n), lambda i: (0, 0))],
        out_specs=pl.BlockSpec((tm, n), lambda i: (i, 0)),
        out_shape=jax.ShapeDtypeStruct((t, n), out_dtype),
        compiler_params=pltpu.CompilerParams(dimension_semantics=("parallel",), vmem_limit_bytes=VMEM_LIMIT),
        name="norm_matmul",
    )(x2d, g.reshape(1, d), w.astype(BF16))


GDN_STEP_CHUNKS = 2
GDN_TOK = CHUNK * GDN_STEP_CHUNKS
CONV_TAIL = 8


def _gdn_kernel(qkv_ref, gate_ref, ba_ref, convw_ref, alog_ref, dtb_ref, ogain_ref, o_ref, xbuf, state):
    c = CHUNK
    tok = GDN_TOK

    @pl.when(pl.program_id(1) == 0)
    def _():
        xbuf[0:CONV_TAIL, :] = jnp.zeros((CONV_TAIL, xbuf.shape[1]), F32)
        state[...] = jnp.zeros(state.shape, F32)

    xbuf[CONV_TAIL:CONV_TAIL + tok, :] = qkv_ref[...]

    def conv_act(lo):
        base = CONV_TAIL - (CONV_K - 1)
        acc = xbuf[base:base + tok, lo:lo + GDN_D] * convw_ref[0:1, lo:lo + GDN_D]
        for j in range(1, CONV_K):
            acc = acc + xbuf[base + j:base + j + tok, lo:lo + GDN_D] * convw_ref[j:j + 1, lo:lo + GDN_D]
        return acc * _sigmoid(acc)

    ba = ba_ref[...]
    beta_all = _sigmoid(ba)
    g_all = -jnp.exp(alog_ref[...]) * _softplus(ba + dtb_ref[...])
    trow = lax.broadcasted_iota(jnp.int32, (tok, tok), 0)
    tcol = lax.broadcasted_iota(jnp.int32, (tok, tok), 1)
    same_chunk = (trow // c) == (tcol // c)
    lower = jnp.where(jnp.logical_and(same_chunk, trow >= tcol), 1.0, 0.0)
    upper = jnp.where(jnp.logical_and(same_chunk, trow <= tcol), 1.0, 0.0)
    gc_all = _dot_f32(lower, g_all)
    gct_all = _dot_f32(g_all.T, upper)
    row = lax.broadcasted_iota(jnp.int32, (c, c), 0)
    col = lax.broadcasted_iota(jnp.int32, (c, c), 1)
    incl = row >= col
    strict = row > col
    eye = (row == col).astype(F32)
    ogain = ogain_ref[...]

    heads = range(GDN_HEADS)
    chunks = range(GDN_STEP_CHUNKS)
    items = [(ci, h) for ci in chunks for h in heads]

    q_full, k_full, v_full = [], [], []
    for h in heads:
        lo = h * GDN_D
        qf = conv_act(lo)
        kf = conv_act(GDN_WIDTH + lo)
        q_full.append(qf * lax.rsqrt(jnp.sum(qf * qf, axis=-1, keepdims=True) + EPS) * (GDN_D ** -0.5))
        k_full.append(kf * lax.rsqrt(jnp.sum(kf * kf, axis=-1, keepdims=True) + EPS))
        v_full.append(conv_act(2 * GDN_WIDTH + lo))

    pre = {}
    for ci, h in items:
        r0 = ci * c
        gl = GDN_HEADS + h
        qh = q_full[h][r0:r0 + c]
        kh = k_full[h][r0:r0 + c]
        vh = v_full[h][r0:r0 + c]
        beta = beta_all[r0:r0 + c, h:h + 1]
        gcol = gc_all[r0:r0 + c, gl:gl + 1]
        grow = gct_all[gl:gl + 1, r0:r0 + c]
        glast = gc_all[r0 + c - 1:r0 + c, gl:gl + 1]
        decay = jnp.where(incl, jnp.exp(jnp.where(incl, gcol - grow, 0.0)), 0.0)
        kb = kh * beta
        khb = kh.astype(BF16)
        egc = jnp.exp(gcol)
        pre[ci, h] = dict(
            p=-jnp.where(strict, _dot_nt(kb.astype(BF16), khb) * decay, 0.0),
            att=jnp.where(incl, _dot_nt(qh.astype(BF16), khb) * decay, 0.0).astype(BF16),
            rhs=jnp.concatenate([vh * beta, kb * egc], axis=1).astype(BF16),
            qg=(qh * egc).astype(BF16),
            ktail=(kh * jnp.exp(glast - gcol)).astype(BF16),
            sdecay=jnp.exp(glast))

    tinv = {it: eye + pre[it]["p"] for it in items}
    pw = {it: pre[it]["p"] for it in items}
    for _ in range(5):
        for it in items:
            pb = pw[it].astype(BF16)
            pw[it] = _dot(pb, pb)
        for it in items:
            tinv[it] = tinv[it] + _dot(tinv[it].astype(BF16), pw[it].astype(BF16))
    sol = {it: _dot(tinv[it].astype(BF16), pre[it]["rhs"]) for it in items}

    s_cur = [state[h] for h in heads]
    for ci in chunks:
        r0 = ci * c
        for h in heads:
            pr = pre[ci, h]
            lo = h * GDN_D
            sb = s_cur[h].astype(BF16)
            v_new = sol[ci, h][:, :GDN_D] - _dot(sol[ci, h][:, GDN_D:].astype(BF16), sb)
            vnb = v_new.astype(BF16)
            o = _dot(pr["qg"], sb) + _dot(pr["att"], vnb)
            s_cur[h] = s_cur[h] * pr["sdecay"] + _dot_tn(pr["ktail"], vnb)
            gate = gate_ref[r0:r0 + c, lo:lo + GDN_D]
            o = o * _rms_scale(o) * ogain * (gate * _sigmoid(gate))
            o_ref[r0:r0 + c, lo:lo + GDN_D] = o.astype(o_ref.dtype)
    for h in heads:
        state[h] = s_cur[h]
    xbuf[0:CONV_TAIL, :] = xbuf[tok:tok + CONV_TAIL, :]


def _gdn(proj, conv_w, a_log, dt_bias, o_gain, bsz, seq):
    t = proj.shape[0]
    n_s = seq // GDN_TOK
    qkv_w = 3 * GDN_WIDTH
    pad = jnp.zeros((GDN_HEADS,), F32)
    alog_row = jnp.concatenate([pad, a_log.astype(F32), jnp.zeros((LANES - 2 * GDN_HEADS,), F32)]).reshape(1, LANES)
    dtb_row = jnp.concatenate([pad, dt_bias.astype(F32), jnp.zeros((LANES - 2 * GDN_HEADS,), F32)]).reshape(1, LANES)
    return pl.pallas_call(
        _gdn_kernel,
        grid=(bsz, n_s),
        in_specs=[pl.BlockSpec((GDN_TOK, qkv_w), lambda b, s: (b * n_s + s, 0)),
                  pl.BlockSpec((GDN_TOK, GDN_WIDTH), lambda b, s: (b * n_s + s, qkv_w // GDN_WIDTH)),
                  pl.BlockSpec((GDN_TOK, LANES), lambda b, s: (b * n_s + s, (A_IN_PAD - LANES) // LANES)),
                  pl.BlockSpec((CONV_K, qkv_w), lambda b, s: (0, 0)),
                  pl.BlockSpec((1, LANES), lambda b, s: (0, 0)),
                  pl.BlockSpec((1, LANES), lambda b, s: (0, 0)),
                  pl.BlockSpec((1, GDN_D), lambda b, s: (0, 0))],
        out_specs=pl.BlockSpec((GDN_TOK, GDN_WIDTH), lambda b, s: (b * n_s + s, 0)),
        out_shape=jax.ShapeDtypeStruct((t, GDN_WIDTH), BF16),
        scratch_shapes=[pltpu.VMEM((GDN_TOK + CONV_TAIL, qkv_w), F32),
                        pltpu.VMEM((GDN_HEADS, GDN_D, GDN_D), F32)],
        compiler_params=pltpu.CompilerParams(dimension_semantics=("parallel", "arbitrary"),
                                             vmem_limit_bytes=VMEM_LIMIT),
        name="gdn",
    )(proj, proj, proj, conv_w.astype(F32), alog_row, dtb_row, o_gain.astype(F32).reshape(1, GDN_D))


def _mix_out_kernel(h_ref, o_ref, mq_ref, mk_ref, mv_ref, wo_ref, wm_ref, out_ref):
    mq = (mq_ref[...] * (MEM_DH ** -0.5)).astype(BF16)
    mk = mk_ref[...]
    mv = mv_ref[...]
    head = lax.broadcasted_iota(jnp.int32, mk.shape, 1) // MEM_DH
    m = jnp.zeros(mq.shape, F32)
    for hh in range(MEM_HEADS):
        sc = _dot_nt(mq, jnp.where(head == hh, mk, 0.0).astype(BF16))
        sc = sc - jnp.max(sc, axis=-1, keepdims=True)
        p = jnp.exp(sc)
        p = p / jnp.sum(p, axis=-1, keepdims=True)
        m = m + _dot(p.astype(BF16), jnp.where(head == hh, mv, 0.0).astype(BF16))
    y = _dot(o_ref[...].astype(BF16), wo_ref[...]) + _dot(m.astype(BF16), wm_ref[...])
    out_ref[...] = h_ref[...] + y


def _mix_out(h2d, o, proj, mq_block, mkv, w_out, bsz, seq, ts):
    t, d = h2d.shape
    ts = min(ts, seq)
    n_s = seq // ts
    m_len = mkv.shape[0] // bsz
    ow = o.shape[1]
    return pl.pallas_call(
        _mix_out_kernel,
        grid=(bsz, n_s),
        in_specs=[pl.BlockSpec((ts, d), lambda b, s: (b * n_s + s, 0)),
                  pl.BlockSpec((ts, ow), lambda b, s: (b * n_s + s, 0)),
                  pl.BlockSpec((ts, MEM_WIDTH), lambda b, s: (b * n_s + s, mq_block)),
                  pl.BlockSpec((m_len, MEM_WIDTH), lambda b, s: (b, 0)),
                  pl.BlockSpec((m_len, MEM_WIDTH), lambda b, s: (b, 1)),
                  pl.BlockSpec((ow, d), lambda b, s: (0, 0)),
                  pl.BlockSpec((MEM_WIDTH, d), lambda b, s: (0, 0))],
        out_specs=pl.BlockSpec((ts, d), lambda b, s: (b * n_s + s, 0)),
        out_shape=jax.ShapeDtypeStruct((t, d), F32),
        compiler_params=pltpu.CompilerParams(dimension_semantics=("parallel", "parallel"),
                                             vmem_limit_bytes=VMEM_LIMIT),
        name="mix_out",
    )(h2d, o, proj, mkv, mkv, w_out[:ow].astype(BF16), w_out[ow:].astype(BF16))


SB_SUB = 4
SB_NEG = -1e30
SB_DONE = 88.0


def _split_bf16(x):
    hi = pltpu.bitcast(pltpu.bitcast(x, jnp.uint32) & jnp.uint32(0xFFFF0000), F32)
    return hi.astype(BF16), (x - hi).astype(BF16)


def _sb_kernel(q_ref, k_ref, v_ref, o_ref):
    blk = SB_BLOCK
    step = pl.program_id(2)
    lane = lax.broadcasted_iota(jnp.int32, (blk, 2 * SB_DH), 1)
    row2 = lax.broadcasted_iota(jnp.int32, (blk, 2 * blk), 0)
    col2 = lax.broadcasted_iota(jnp.int32, (blk, 2 * blk), 1)
    causal2 = col2 < row2 + blk
    srow2 = lax.broadcasted_iota(jnp.int32, (2 * blk, 2 * blk), 0)
    scol2 = lax.broadcasted_iota(jnp.int32, (2 * blk, 2 * blk), 1)
    suffix2 = (srow2 >= scol2).astype(BF16)
    suffix1 = suffix2[:blk, :blk]

    def suffix_sums(sp, suffix):
        hi, lo = _split_bf16(sp)
        return _dot(hi, suffix) + _dot(lo, suffix)

    q_heads, k2s, v2s = [], [], []
    for j in range(SB_SUB):
        i = step * SB_SUB + j
        q = q_ref[j * blk:(j + 1) * blk, :] * (SB_DH ** -0.5)
        q_heads.append(jnp.where(lane < SB_DH, q, 0.0).astype(BF16))
        q_heads.append(jnp.where(lane >= SB_DH, q, 0.0).astype(BF16))
        prev = pl.multiple_of(jnp.maximum(i - 1, 0) * blk, blk)
        diag = pl.multiple_of(i * blk, blk)
        k2s.append(jnp.concatenate([k_ref[pl.ds(prev, blk), :], k_ref[pl.ds(diag, blk), :]], axis=0))
        has_prev = jnp.where(i > 0, 1.0, 0.0).astype(BF16)
        v2s.append(jnp.concatenate([v_ref[pl.ds(prev, blk), :] * has_prev, v_ref[pl.ds(diag, blk), :]], axis=0))
    chains = range(2 * SB_SUB)
    zs = [jnp.where(causal2, _dot_nt(q_heads[n], k2s[n // 2]), SB_NEG) for n in chains]
    splits = [_split_bf16(_softplus(zs[n])) for n in chains]
    sums = [_dot(splits[n][0], suffix2) + _dot(splits[n][1], suffix2) for n in chains]
    probs = [jnp.exp(zs[n] - sums[n]).astype(BF16) for n in chains]
    carries = [sums[n][:, 0:1] for n in chains]
    accs = [_dot(probs[n], v2s[n // 2]) for n in chains]

    def lowest(cs):
        m = cs[0]
        for cc in cs[1:]:
            m = jnp.minimum(m, cc)
        return jnp.min(m)

    def more(st):
        it, cmin = st[0], st[1]
        return jnp.logical_and(it < (step + 1) * SB_SUB - 2, cmin < SB_DONE)

    def body(st):
        it = st[0]
        cs = list(st[2:2 + 2 * SB_SUB])
        acs = list(st[2 + 2 * SB_SUB:])
        kblks, vblks = [], []
        for j in range(SB_SUB):
            kb = step * SB_SUB + j - 2 - it
            start = pl.multiple_of(jnp.maximum(kb, 0) * blk, blk)
            kblks.append(k_ref[pl.ds(start, blk), :])
            vblks.append(v_ref[pl.ds(start, blk), :] * jnp.where(kb >= 0, 1.0, 0.0).astype(BF16))
        zl = [_dot_nt(q_heads[n], kblks[n // 2]) for n in chains]
        sl = [suffix_sums(_softplus(zl[n]), suffix1) for n in chains]
        pl_ = [jnp.exp(zl[n] - cs[n] - sl[n]).astype(BF16) for n in chains]
        acs = [acs[n] + _dot(pl_[n], vblks[n // 2]) for n in chains]
        cs = [cs[n] + sl[n][:, 0:1] for n in chains]
        return (it + 1, lowest(cs), *cs, *acs)

    st = lax.while_loop(more, body, (jnp.int32(0), lowest(carries), *carries, *accs))
    accs = st[2 + 2 * SB_SUB:]
    for j in range(SB_SUB):
        o_ref[j * blk:(j + 1) * blk, :] = jnp.where(lane < SB_DH, accs[2 * j], accs[2 * j + 1]).astype(o_ref.dtype)


def _stick_breaking(qproj, kv, bsz, seq):
    t = qproj.shape[0]
    tq = SB_SUB * SB_BLOCK
    n_q = seq // tq
    pairs = SB_HEADS // 2
    pw = 2 * SB_DH
    return pl.pallas_call(
        _sb_kernel,
        grid=(bsz, pairs, n_q),
        in_specs=[pl.BlockSpec((tq, pw), lambda b, p, i: (b * n_q + i, p)),
                  pl.BlockSpec((seq, pw), lambda b, p, i: (b, p)),
                  pl.BlockSpec((seq, pw), lambda b, p, i: (b, pairs + p))],
        out_specs=pl.BlockSpec((tq, pw), lambda b, p, i: (b * n_q + i, p)),
        out_shape=jax.ShapeDtypeStruct((t, SB_WIDTH), BF16),
        compiler_params=pltpu.CompilerParams(dimension_semantics=("parallel", "parallel", "arbitrary"),
                                             vmem_limit_bytes=VMEM_LIMIT),
        name="stick_breaking",
    )(qproj, kv, kv)


MOE_TM = 1024
MOE_SUB = 320


def _split3(x):
    hi = x.astype(BF16)
    r1 = x - hi.astype(F32)
    mid = r1.astype(BF16)
    return hi, mid, (r1 - mid.astype(F32)).astype(BF16)


def _moe_kernel(h_ref, g_ref, wrh_ref, wrl_ref, br_ref, sel_ref, w13_ref, w2_ref, fg_ref, out_ref,
                xn_sc, cw_sc, mcol_sc, mrow_sc, cnt_sc, acc_sc, tril_sc, triu_sc, *, final_norm):
    tile = pl.program_id(0)
    grp = pl.program_id(1)
    tm = h_ref.shape[0]

    @pl.when(jnp.logical_and(tile == 0, grp == 0))
    def _():
        r = lax.broadcasted_iota(jnp.int32, (tm, tm), 0)
        c = lax.broadcasted_iota(jnp.int32, (tm, tm), 1)
        tril_sc[...] = jnp.where(r > c, 1.0, 0.0).astype(BF16)
        triu_sc[...] = jnp.where(r < c, 1.0, 0.0).astype(BF16)

    @pl.when(grp == 0)
    def _():
        x = h_ref[...]
        xn = x * _rms_scale(x) * g_ref[...]
        xh = xn.astype(BF16)
        xn_sc[...] = xh
        xl = (xn - xh.astype(F32)).astype(BF16)
        logits = _dot(xh, wrh_ref[...]) + _dot(xl, wrh_ref[...]) + _dot(xh, wrl_ref[...]) + br_ref[...]
        lane = lax.broadcasted_iota(jnp.int32, (tm, LANES), 1)
        lane_f = lane.astype(F32)
        big = float(LANES)
        is_g = lane < N_GROUPS
        gl = jnp.where(is_g, logits, -jnp.inf)
        gmax = jnp.max(gl, axis=-1, keepdims=True)
        p_group = 1.0 / jnp.sum(jnp.where(is_g, jnp.exp(logits - gmax), 0.0), axis=-1, keepdims=True)
        gsel = jnp.min(jnp.where(gl == gmax, lane_f, big), axis=-1, keepdims=True)
        e_lane = lane - ROUTER_LANE0
        grp_of_lane = (e_lane // EXPERTS_PER_GROUP).astype(F32)
        in_grp = jnp.logical_and(jnp.logical_and(e_lane >= 0, e_lane < N_EXPERTS), grp_of_lane == gsel)
        el = jnp.where(in_grp, logits, -jnp.inf)
        emax = jnp.max(el, axis=-1, keepdims=True)
        ee = jnp.where(in_grp, jnp.exp(logits - emax), 0.0)
        prob = ee / jnp.sum(ee, axis=-1, keepdims=True)
        pm = jnp.where(in_grp, prob, -1.0)
        p1 = jnp.max(pm, axis=-1, keepdims=True)
        i1 = jnp.min(jnp.where(pm == p1, lane_f, big), axis=-1, keepdims=True)
        pm2 = jnp.where(lane_f == i1, -1.0, pm)
        p2 = jnp.max(pm2, axis=-1, keepdims=True)
        i2 = jnp.min(jnp.where(pm2 == p2, lane_f, big), axis=-1, keepdims=True)
        sel = jnp.logical_or(lane_f == i1, lane_f == i2)
        comb = jnp.where(sel, p_group * (prob / (p1 + p2)), 0.0)
        ch, cm, cl = _split3(comb)
        lane_sel = sel_ref[...]
        comb4 = _dot(ch, lane_sel) + _dot(cm, lane_sel) + _dot(cl, lane_sel)
        c4h, c4m, c4l = _split3(comb4)
        cw_sc[:, 0:LANES] = c4h
        cw_sc[:, LANES:2 * LANES] = c4m
        cw_sc[:, 2 * LANES:3 * LANES] = c4l
        onehot = jnp.where(lane_f == gsel, 1.0, 0.0)
        rank_col = jnp.sum(onehot * _dot(tril_sc[...], onehot.astype(BF16)), axis=-1, keepdims=True)
        mcol_sc[...] = jnp.where(lane == 0, gsel, jnp.where(lane == 1, rank_col, 0.0))
        cnt_sc[...] = jnp.broadcast_to(jnp.sum(onehot, axis=0, keepdims=True), cnt_sc.shape)
        oht = onehot.T
        cumt = _dot(oht.astype(BF16), triu_sc[...])
        sub = lax.broadcasted_iota(jnp.int32, (LANES, tm), 0)
        rank_row = jnp.sum(oht * cumt, axis=0, keepdims=True)
        grp_row = jnp.sum(oht * sub.astype(F32), axis=0, keepdims=True)
        srow = lax.broadcasted_iota(jnp.int32, mrow_sc.shape, 0)
        mrow_sc[...] = jnp.where(srow == 0, grp_row, jnp.where(srow == 1, rank_row, 0.0))
        acc_sc[...] = jnp.zeros(acc_sc.shape, F32)

    grp_f = grp.astype(F32)
    cnt_lane = lax.broadcasted_iota(jnp.int32, (1, LANES), 1)
    n_rows = jnp.sum(jnp.where(cnt_lane == grp, cnt_sc[0:1, :], 0.0)).astype(jnp.int32)
    rel_row = jnp.where(mrow_sc[0:1, :] == grp_f, mrow_sc[1:2, :], -1.0)
    mcol = mcol_sc[...]
    rel_col = jnp.where(mcol[:, 0:1] == grp_f, mcol[:, 1:2], -1.0)
    r_iota = lax.broadcasted_iota(jnp.int32, (MOE_SUB, tm), 0).astype(F32)
    c_iota = lax.broadcasted_iota(jnp.int32, (tm, MOE_SUB), 1).astype(F32)

    def sub_tile(s, carry):
        base = (s * MOE_SUB).astype(F32)
        gather = jnp.where(r_iota + base == rel_row, 1.0, 0.0).astype(BF16)
        xs = _dot(gather, xn_sc[...]).astype(BF16)
        cw3 = _dot(gather, cw_sc[...])
        cw = cw3[:, 0:LANES] + cw3[:, LANES:2 * LANES] + cw3[:, 2 * LANES:3 * LANES]
        y = jnp.zeros((MOE_SUB, out_ref.shape[1]), F32)
        for e in range(EXPERTS_PER_GROUP):
            hcat = _dot(xs, w13_ref[e])
            hg = hcat[:, :EXPERT_FF]
            hid = hg * _sigmoid(hg) * hcat[:, EXPERT_FF:]
            y = y + _dot((hid * cw[:, e:e + 1]).astype(BF16), w2_ref[e])
        scatter = jnp.where(c_iota + base == rel_col, 1.0, 0.0).astype(BF16)
        acc_sc[...] += _dot(scatter, y.astype(BF16))
        return carry

    lax.fori_loop(0, (n_rows + MOE_SUB - 1) // MOE_SUB, sub_tile, 0)

    @pl.when(grp == N_GROUPS - 1)
    def _():
        y = h_ref[...] + acc_sc[...]
        if final_norm:
            y = y * _rms_scale(y) * fg_ref[...]
        out_ref[...] = y


def _moe(h2d, ffn_g, w_group, b_group, w_router, b_router, w1, w3, w2, final_g, final_norm):
    t, d = h2d.shape
    tm = min(MOE_TM, t)
    pad = LANES - N_GROUPS - N_EXPERTS
    wr = jnp.concatenate([w_group, w_router, jnp.zeros((d, pad), F32)], axis=1).astype(F32)
    wr_hi = wr.astype(BF16)
    wr_lo = (wr - wr_hi.astype(F32)).astype(BF16)
    br = jnp.concatenate([b_group, b_router, jnp.zeros((pad,), F32)]).astype(F32).reshape(1, LANES)
    src = jnp.arange(LANES)[:, None]
    dst = jnp.arange(LANES)[None, :]
    is_expert = jnp.logical_and(src >= ROUTER_LANE0, src < ROUTER_LANE0 + N_EXPERTS)
    lane_sel = jnp.logical_and(is_expert, (src - ROUTER_LANE0) % EXPERTS_PER_GROUP == dst).astype(BF16)
    w13 = jnp.concatenate([w1, w3], axis=-1).astype(BF16)
    const = lambda i, g: (0, 0)
    return pl.pallas_call(
        functools.partial(_moe_kernel, final_norm=final_norm),
        grid=(t // tm, N_GROUPS),
        in_specs=[pl.BlockSpec((tm, d), lambda i, g: (i, 0)),
                  pl.BlockSpec((1, d), const),
                  pl.BlockSpec((d, LANES), const),
                  pl.BlockSpec((d, LANES), const),
                  pl.BlockSpec((1, LANES), const),
                  pl.BlockSpec((LANES, LANES), const),
                  pl.BlockSpec((EXPERTS_PER_GROUP, d, 2 * EXPERT_FF), lambda i, g: (g, 0, 0)),
                  pl.BlockSpec((EXPERTS_PER_GROUP, EXPERT_FF, d), lambda i, g: (g, 0, 0)),
                  pl.BlockSpec((1, d), const)],
        out_specs=pl.BlockSpec((tm, d), lambda i, g: (i, 0)),
        out_shape=jax.ShapeDtypeStruct((t, d), F32),
        scratch_shapes=[pltpu.VMEM((tm, d), BF16),
                        pltpu.VMEM((tm, 3 * LANES), BF16),
                        pltpu.VMEM((tm, LANES), F32),
                        pltpu.VMEM((8, tm), F32),
                        pltpu.VMEM((8, LANES), F32),
                        pltpu.VMEM((tm, d), F32),
                        pltpu.VMEM((tm, tm), BF16), pltpu.VMEM((tm, tm), BF16)],
        compiler_params=pltpu.CompilerParams(dimension_semantics=("arbitrary", "arbitrary"),
                                             vmem_limit_bytes=VMEM_LIMIT),
        name="moe",
    )(h2d, ffn_g.reshape(1, d), wr_hi, wr_lo, br, lane_sel, w13, w2.astype(BF16), final_g.reshape(1, d))


def kernel(x, mem, a_norm, a_w_in, a_conv, a_log, a_dt_bias, a_out_gain, a_w_out, kv_norm, w_kv, b_norm, b_w_in,
           b_w_out, mem_norm, w_mem_kv, ffn_norm, w_group, b_group, w_router, b_router, w1, w3, w2, final_norm):
    bsz, seq, d = x.shape
    m_len = mem.shape[1]
    depth = mem_norm.shape[0]
    n_a = a_norm.shape[0]
    h = x.reshape(bsz * seq, d)
    mem2d = mem.reshape(bsz * m_len, d)
    kv = None
    for l in range(depth):
        mkv = _norm_matmul(mem2d, mem_norm[l], w_mem_kv[l], F32, 256)
        if l < n_a:
            w_in = a_w_in[l]
            g4 = 4 * GDN_WIDTH
            w_perm = jnp.concatenate([w_in[:, :g4], w_in[:, g4 + 2 * GDN_HEADS:], w_in[:, g4:g4 + 2 * GDN_HEADS],
                                      jnp.zeros((d, LANES - 2 * GDN_HEADS), w_in.dtype)], axis=1)
            proj = _norm_matmul(h, a_norm[l], w_perm, F32, 256)
            o = _gdn(proj, a_conv[l], a_log[l], a_dt_bias[l], a_out_gain[l], bsz, seq)
            h = _mix_out(h, o, proj, g4 // MEM_WIDTH, mkv, a_w_out[l], bsz, seq, 256)
        else:
            lb = l - n_a
            if l == n_a:
                kv = _norm_matmul(h, kv_norm, w_kv, BF16, 256)
            proj = _norm_matmul(h, b_norm[lb], b_w_in[lb], F32, 256)
            o = _stick_breaking(proj, kv, bsz, seq)
            h = _mix_out(h, o, proj, SB_WIDTH // MEM_WIDTH, mkv, b_w_out[lb], bsz, seq, 256)
        h = _moe(h, ffn_norm[l], w_group[l], b_group[l], w_router[l], b_router[l], w1[l], w3[l], w2[l],
                 final_norm, l == depth - 1)
    return h.reshape(bsz, seq, d)
```

```python
import functools

import jax
import jax.numpy as jnp
from jax import lax
from jax.experimental import pallas as pl
from jax.experimental.pallas import tpu as pltpu

EPS = 1e-6
GDN_HEADS = 6
GDN_D = 128
GDN_WIDTH = GDN_HEADS * GDN_D
CONV_K = 4
CHUNK = 64
SB_HEADS = 12
SB_DH = 64
SB_WIDTH = SB_HEADS * SB_DH
SB_BLOCK = 128
MEM_HEADS = 4
MEM_DH = 64
MEM_WIDTH = MEM_HEADS * MEM_DH
N_GROUPS = 4
EXPERTS_PER_GROUP = 4
N_EXPERTS = N_GROUPS * EXPERTS_PER_GROUP
EXPERT_FF = 256
LANES = 128
A_IN_PAD = 4 * GDN_WIDTH + MEM_WIDTH + LANES
ROUTER_LANE0 = N_GROUPS
VMEM_LIMIT = 48 * 1024 * 1024
PROJ_TM = 512
MIX_TS = 256

F32 = jnp.float32
BF16 = jnp.bfloat16


def _dot(a, b):
    return jnp.dot(a, b, preferred_element_type=F32)


def _dot_nt(a, b):
    return lax.dot_general(a, b, (((1,), (1,)), ((), ())), preferred_element_type=F32)


def _dot_f32(a, b):
    return jnp.dot(a, b, preferred_element_type=F32, precision=lax.Precision.HIGHEST)


def _sigmoid(x):
    return 1.0 / (1.0 + jnp.exp(-x))


def _softplus(x):
    return jnp.maximum(x, 0.0) + jnp.log(1.0 + jnp.exp(-jnp.abs(x)))


def _rms_scale(x):
    return lax.rsqrt(jnp.mean(x * x, axis=-1, keepdims=True) + EPS)


def _norm_matmul_kernel(x_ref, g_ref, w_ref, o_ref, *, col_chunk):
    x = x_ref[...]
    xn = (x * _rms_scale(x) * g_ref[...]).astype(BF16)
    n = o_ref.shape[-1]
    for c0 in range(0, n, col_chunk):
        c1 = min(n, c0 + col_chunk)
        o_ref[:, c0:c1] = _dot(xn, w_ref[:, c0:c1]).astype(o_ref.dtype)


def _norm_matmul(x2d, g, w, out_dtype, tm):
    t, d = x2d.shape
    n = w.shape[1]
    tm = min(tm, t)
    return pl.pallas_call(
        functools.partial(_norm_matmul_kernel, col_chunk=4 * LANES),
        grid=(t // tm,),
        in_specs=[pl.BlockSpec((tm, d), lambda i: (i, 0)),
                  pl.BlockSpec((1, d), lambda i: (0, 0)),
                  pl.BlockSpec((d, n), lambda i: (0, 0))],
        out_specs=pl.BlockSpec((tm, n), lambda i: (i, 0)),
        out_shape=jax.ShapeDtypeStruct((t, n), out_dtype),
        compiler_params=pltpu.CompilerParams(dimension_semantics=("parallel",), vmem_limit_bytes=VMEM_LIMIT),
        name="norm_matmul",
    )(x2d, g.reshape(1, d), w.astype(BF16))


GDN_STEP_CHUNKS = 4
GDN_TOK = CHUNK * GDN_STEP_CHUNKS
CONV_TAIL = 8


def _gdn_kernel(qkv_ref, gate_ref, ba_ref, convw_ref, alog_ref, dtb_ref, ogain_ref, o_ref, xbuf, state):
    c = CHUNK
    tok = GDN_TOK

    @pl.when(pl.program_id(1) == 0)
    def _():
        xbuf[0:CONV_TAIL, :] = jnp.zeros((CONV_TAIL, xbuf.shape[1]), F32)
        state[...] = jnp.zeros(state.shape, F32)

    xbuf[CONV_TAIL:CONV_TAIL + tok, :] = qkv_ref[...]

    def conv_act(lo):
        win = xbuf[:, lo:lo + GDN_D]
        acc = win[CONV_TAIL:] * convw_ref[CONV_K - 1:CONV_K, lo:lo + GDN_D]
        for j in range(CONV_K - 1):
            shifted = pltpu.roll(win, CONV_K - 1 - j, axis=0)[CONV_TAIL:]
            acc = acc + shifted * convw_ref[j:j + 1, lo:lo + GDN_D]
        return acc * _sigmoid(acc)

    ba = ba_ref[...]
    beta_all = _sigmoid(ba)
    g_all = -jnp.exp(alog_ref[...]) * _softplus(ba + dtb_ref[...])
    trow = lax.broadcasted_iota(jnp.int32, (tok, tok), 0)
    tcol = lax.broadcasted_iota(jnp.int32, (tok, tok), 1)
    same_chunk = (trow // c) == (tcol // c)
    lower = jnp.where(jnp.logical_and(same_chunk, trow >= tcol), 1.0, 0.0)
    upper = jnp.where(jnp.logical_and(same_chunk, trow <= tcol), 1.0, 0.0)
    gc_all = _dot_f32(lower, g_all)
    gct_all = _dot_f32(g_all.T, upper)
    row = lax.broadcasted_iota(jnp.int32, (c, c), 0)
    col = lax.broadcasted_iota(jnp.int32, (c, c), 1)
    incl = row >= col
    strict = row > col
    eye = (row == col).astype(F32)
    ogain = ogain_ref[...]

    heads = range(GDN_HEADS)
    chunks = range(GDN_STEP_CHUNKS)
    items = [(ci, h) for ci in chunks for h in heads]

    q_full, k_full, v_full = [], [], []
    for h in heads:
        lo = h * GDN_D
        qf = conv_act(lo)
        kf = conv_act(GDN_WIDTH + lo)
        q_full.append(qf * lax.rsqrt(jnp.sum(qf * qf, axis=-1, keepdims=True) + EPS) * (GDN_D ** -0.5))
        k_full.append(kf * lax.rsqrt(jnp.sum(kf * kf, axis=-1, keepdims=True) + EPS))
        v_full.append(conv_act(2 * GDN_WIDTH + lo))

    pre = {}
    for ci, h in items:
        r0 = ci * c
        gl = GDN_HEADS + h
        qh = q_full[h][r0:r0 + c]
        kh = k_full[h][r0:r0 + c]
        vh = v_full[h][r0:r0 + c]
        beta = beta_all[r0:r0 + c, h:h + 1]
        gcol = gc_all[r0:r0 + c, gl:gl + 1]
        grow = gct_all[gl:gl + 1, r0:r0 + c]
        glast = gc_all[r0 + c - 1:r0 + c, gl:gl + 1]
        decay = jnp.where(incl, jnp.exp(jnp.where(incl, gcol - grow, 0.0)), 0.0)
        kb = kh * beta
        khb = kh.astype(BF16)
        egc = jnp.exp(gcol)
        pre[ci, h] = dict(
            p=-jnp.where(strict, _dot_nt(kb.astype(BF16), khb) * decay, 0.0),
            att=jnp.where(incl, _dot_nt(qh.astype(BF16), khb) * decay, 0.0).astype(BF16),
            rhs=jnp.concatenate([vh * beta, kb * egc], axis=1).astype(BF16),
            qg=qh * egc,
            ktail_t=(kh * jnp.exp(glast - gcol)).T.astype(BF16),
            sdecay=jnp.exp(glast))

    tinv = {it: eye + pre[it]["p"] for it in items}
    pw = {it: pre[it]["p"] for it in items}
    for _ in range(5):
        for it in items:
            pb = pw[it].astype(BF16)
            pw[it] = _dot(pb, pb)
        for it in items:
            tinv[it] = tinv[it] + _dot(tinv[it].astype(BF16), pw[it].astype(BF16))
    sol = {it: _dot(tinv[it].astype(BF16), pre[it]["rhs"]) for it in items}

    drow = lax.broadcasted_iota(jnp.int32, (GDN_D, GDN_D), 0)
    dcol = lax.broadcasted_iota(jnp.int32, (GDN_D, GDN_D), 1)
    eye_d = (drow == dcol).astype(F32)
    lin = {}
    for it in items:
        pr = pre[it]
        solb = sol[it].astype(BF16)
        att_sol = _dot(pr["att"], solb)
        kt_sol = _dot(pr["ktail_t"], solb)
        lin[it] = ((pr["qg"] - att_sol[:, GDN_D:]).astype(BF16), att_sol[:, :GDN_D],
                   (eye_d * pr["sdecay"] - kt_sol[:, GDN_D:]).astype(BF16), kt_sol[:, :GDN_D])
    s_cur = [state[h] for h in heads]
    for ci in chunks:
        r0 = ci * c
        for h in heads:
            q_lin, o_const, s_lin, s_const = lin[ci, h]
            lo = h * GDN_D
            sb = s_cur[h].astype(BF16)
            o = _dot(q_lin, sb) + o_const
            s_cur[h] = _dot(s_lin, sb) + s_const
            gate = gate_ref[r0:r0 + c, lo:lo + GDN_D]
            o = o * _rms_scale(o) * ogain * (gate * _sigmoid(gate))
            o_ref[r0:r0 + c, lo:lo + GDN_D] = o.astype(o_ref.dtype)
    for h in heads:
        state[h] = s_cur[h]
    xbuf[0:CONV_TAIL, :] = xbuf[tok:tok + CONV_TAIL, :]


def _gdn(proj, conv_w, a_log, dt_bias, o_gain, bsz, seq):
    t = proj.shape[0]
    n_s = seq // GDN_TOK
    qkv_w = 3 * GDN_WIDTH
    pad = jnp.zeros((GDN_HEADS,), F32)
    alog_row = jnp.concatenate([pad, a_log.astype(F32), jnp.zeros((LANES - 2 * GDN_HEADS,), F32)]).reshape(1, LANES)
    dtb_row = jnp.concatenate([pad, dt_bias.astype(F32), jnp.zeros((LANES - 2 * GDN_HEADS,), F32)]).reshape(1, LANES)
    return pl.pallas_call(
        _gdn_kernel,
        grid=(bsz, n_s),
        in_specs=[pl.BlockSpec((GDN_TOK, qkv_w), lambda b, s: (b * n_s + s, 0)),
                  pl.BlockSpec((GDN_TOK, GDN_WIDTH), lambda b, s: (b * n_s + s, qkv_w // GDN_WIDTH)),
                  pl.BlockSpec((GDN_TOK, LANES), lambda b, s: (b * n_s + s, (A_IN_PAD - LANES) // LANES)),
                  pl.BlockSpec((CONV_K, qkv_w), lambda b, s: (0, 0)),
                  pl.BlockSpec((1, LANES), lambda b, s: (0, 0)),
                  pl.BlockSpec((1, LANES), lambda b, s: (0, 0)),
                  pl.BlockSpec((1, GDN_D), lambda b, s: (0, 0))],
        out_specs=pl.BlockSpec((GDN_TOK, GDN_WIDTH), lambda b, s: (b * n_s + s, 0)),
        out_shape=jax.ShapeDtypeStruct((t, GDN_WIDTH), BF16),
        scratch_shapes=[pltpu.VMEM((GDN_TOK + CONV_TAIL, qkv_w), F32),
                        pltpu.VMEM((GDN_HEADS, GDN_D, GDN_D), F32)],
        compiler_params=pltpu.CompilerParams(dimension_semantics=("parallel", "arbitrary"),
                                             vmem_limit_bytes=VMEM_LIMIT),
        name="gdn",
    )(proj, proj, proj, conv_w.astype(F32), alog_row, dtb_row, o_gain.astype(F32).reshape(1, GDN_D))


def _mix_out_kernel(h_ref, o_ref, mq_ref, mk_ref, mv_ref, wo_ref, wm_ref, out_ref):
    mq = (mq_ref[...] * (MEM_DH ** -0.5)).astype(BF16)
    mk = mk_ref[...]
    mv = mv_ref[...]
    head = lax.broadcasted_iota(jnp.int32, mk.shape, 1) // MEM_DH
    heads = range(MEM_HEADS)
    k_heads = [jnp.where(head == hh, mk, 0.0).astype(BF16) for hh in heads]
    v_heads = [jnp.where(head == hh, mv, 0.0).astype(BF16) for hh in heads]
    scores = [_dot_nt(mq, k_heads[hh]) for hh in heads]
    probs = []
    for sc in scores:
        p = jnp.exp(sc - jnp.max(sc, axis=-1, keepdims=True))
        probs.append((p * (1.0 / jnp.sum(p, axis=-1, keepdims=True))).astype(BF16))
    m = _dot(probs[0], v_heads[0])
    for hh in heads[1:]:
        m = m + _dot(probs[hh], v_heads[hh])
    y = _dot(o_ref[...].astype(BF16), wo_ref[...]) + _dot(m.astype(BF16), wm_ref[...])
    out_ref[...] = h_ref[...] + y


def _mix_out(h2d, o, proj, mq_block, mkv, w_out, bsz, seq, ts):
    t, d = h2d.shape
    ts = min(ts, seq)
    n_s = seq // ts
    m_len = mkv.shape[0] // bsz
    ow = o.shape[1]
    return pl.pallas_call(
        _mix_out_kernel,
        grid=(bsz, n_s),
        in_specs=[pl.BlockSpec((ts, d), lambda b, s: (b * n_s + s, 0)),
                  pl.BlockSpec((ts, ow), lambda b, s: (b * n_s + s, 0)),
                  pl.BlockSpec((ts, MEM_WIDTH), lambda b, s: (b * n_s + s, mq_block)),
                  pl.BlockSpec((m_len, MEM_WIDTH), lambda b, s: (b, 0)),
                  pl.BlockSpec((m_len, MEM_WIDTH), lambda b, s: (b, 1)),
                  pl.BlockSpec((ow, d), lambda b, s: (0, 0)),
                  pl.BlockSpec((MEM_WIDTH, d), lambda b, s: (0, 0))],
        out_specs=pl.BlockSpec((ts, d), lambda b, s: (b * n_s + s, 0)),
        out_shape=jax.ShapeDtypeStruct((t, d), F32),
        compiler_params=pltpu.CompilerParams(dimension_semantics=("parallel", "parallel"),
                                             vmem_limit_bytes=VMEM_LIMIT),
        name="mix_out",
    )(h2d, o, proj, mkv, mkv, w_out[:ow].astype(BF16), w_out[ow:].astype(BF16))


SB_SUB = 4
SB_NEG = -1e30
SB_DONE = 88.0


def _split_bf16(x):
    hi = pltpu.bitcast(pltpu.bitcast(x, jnp.uint32) & jnp.uint32(0xFFFF0000), F32)
    return hi.astype(BF16), (x - hi).astype(BF16)


def _sb_kernel(q_ref, k_ref, v_ref, o_ref):
    blk = SB_BLOCK
    step = pl.program_id(2)
    lane = lax.broadcasted_iota(jnp.int32, (blk, 2 * SB_DH), 1)
    row1 = lax.broadcasted_iota(jnp.int32, (blk, blk), 0)
    col1 = lax.broadcasted_iota(jnp.int32, (blk, blk), 1)
    causal = col1 < row1
    srow2 = lax.broadcasted_iota(jnp.int32, (2 * blk, 2 * blk), 0)
    scol2 = lax.broadcasted_iota(jnp.int32, (2 * blk, 2 * blk), 1)
    suffix2 = (srow2 >= scol2).astype(BF16)
    suffix1 = suffix2[:blk, :blk]
    suffix2x2 = jnp.concatenate([suffix2, suffix2], axis=0)
    suffix1x2 = jnp.concatenate([suffix1, suffix1], axis=0)

    def suffix_sums(sp, suffix_x2):
        return _dot(jnp.concatenate(_split_bf16(sp), axis=1), suffix_x2)

    q_heads, k2s, v2s = [], [], []
    for j in range(SB_SUB):
        i = step * SB_SUB + j
        q = q_ref[j * blk:(j + 1) * blk, :] * (SB_DH ** -0.5)
        q_heads.append(jnp.where(lane < SB_DH, q, 0.0).astype(BF16))
        q_heads.append(jnp.where(lane >= SB_DH, q, 0.0).astype(BF16))
        prev = pl.multiple_of(jnp.maximum(i - 1, 0) * blk, blk)
        diag = pl.multiple_of(i * blk, blk)
        k2s.append(jnp.concatenate([k_ref[pl.ds(prev, blk), :], k_ref[pl.ds(diag, blk), :]], axis=0))
        has_prev = jnp.where(i > 0, 1.0, 0.0).astype(BF16)
        v2s.append(jnp.concatenate([v_ref[pl.ds(prev, blk), :] * has_prev, v_ref[pl.ds(diag, blk), :]], axis=0))
    chains = range(2 * SB_SUB)
    raw = [_dot_nt(q_heads[n], k2s[n // 2]) for n in chains]
    zs = [jnp.concatenate([raw[n][:, :blk], jnp.where(causal, raw[n][:, blk:], SB_NEG)], axis=1) for n in chains]
    sums = [suffix_sums(_softplus(zs[n]), suffix2x2) for n in chains]
    probs = [jnp.exp(zs[n] - sums[n]).astype(BF16) for n in chains]
    carries = [sums[n][:, 0:1] for n in chains]
    accs = [_dot(probs[n], v2s[n // 2]) for n in chains]

    def lowest(cs):
        m = cs[0]
        for cc in cs[1:]:
            m = jnp.minimum(m, cc)
        return jnp.min(m)

    def more(st):
        it, cmin = st[0], st[1]
        return jnp.logical_and(it < (step + 1) * SB_SUB - 2, cmin < SB_DONE)

    def body(st):
        it = st[0]
        cs = list(st[2:2 + 2 * SB_SUB])
        acs = list(st[2 + 2 * SB_SUB:])
        kblks, vblks = [], []
        for j in range(SB_SUB):
            kb = step * SB_SUB + j - 2 - it
            start = pl.multiple_of(jnp.maximum(kb, 0) * blk, blk)
            kblks.append(k_ref[pl.ds(start, blk), :])
            vblks.append(v_ref[pl.ds(start, blk), :] * jnp.where(kb >= 0, 1.0, 0.0).astype(BF16))
        zl = [_dot_nt(q_heads[n], kblks[n // 2]) for n in chains]
        sl = [suffix_sums(_softplus(zl[n]), suffix1x2) for n in chains]
        pl_ = [jnp.exp(zl[n] - cs[n] - sl[n]).astype(BF16) for n in chains]
        acs = [acs[n] + _dot(pl_[n], vblks[n // 2]) for n in chains]
        cs = [cs[n] + sl[n][:, 0:1] for n in chains]
        return (it + 1, lowest(cs), *cs, *acs)

    st = lax.while_loop(more, body, (jnp.int32(0), lowest(carries), *carries, *accs))
    accs = st[2 + 2 * SB_SUB:]
    for j in range(SB_SUB):
        o_ref[j * blk:(j + 1) * blk, :] = jnp.where(lane < SB_DH, accs[2 * j], accs[2 * j + 1]).astype(o_ref.dtype)


def _stick_breaking(qproj, kv, bsz, seq):
    t = qproj.shape[0]
    tq = SB_SUB * SB_BLOCK
    n_q = seq // tq
    pairs = SB_HEADS // 2
    pw = 2 * SB_DH
    return pl.pallas_call(
        _sb_kernel,
        grid=(bsz, pairs, n_q),
        in_specs=[pl.BlockSpec((tq, pw), lambda b, p, i: (b * n_q + i, p)),
                  pl.BlockSpec((seq, pw), lambda b, p, i: (b, p)),
                  pl.BlockSpec((seq, pw), lambda b, p, i: (b, pairs + p))],
        out_specs=pl.BlockSpec((tq, pw), lambda b, p, i: (b * n_q + i, p)),
        out_shape=jax.ShapeDtypeStruct((t, SB_WIDTH), BF16),
        compiler_params=pltpu.CompilerParams(dimension_semantics=("parallel", "parallel", "arbitrary"),
                                             vmem_limit_bytes=VMEM_LIMIT),
        name="stick_breaking",
    )(qproj, kv, kv)


MOE_TM = 1024
MOE_SUB = 320


def _split3(x):
    hi = x.astype(BF16)
    r1 = x - hi.astype(F32)
    mid = r1.astype(BF16)
    return hi, mid, (r1 - mid.astype(F32)).astype(BF16)


def _moe_kernel(h_ref, g_ref, wrh_ref, wrl_ref, br_ref, sel_ref, w13_ref, w2_ref, fg_ref, out_ref,
                xn_sc, cw_sc, mcol_sc, mrow_sc, cnt_sc, acc_sc, tril_sc, triu_sc, *, final_norm):
    tile = pl.program_id(0)
    grp = pl.program_id(1)
    tm = h_ref.shape[0]

    @pl.when(jnp.logical_and(tile == 0, grp == 0))
    def _():
        r = lax.broadcasted_iota(jnp.int32, (tm, tm), 0)
        c = lax.broadcasted_iota(jnp.int32, (tm, tm), 1)
        tril_sc[...] = jnp.where(r > c, 1.0, 0.0).astype(BF16)
        triu_sc[...] = jnp.where(r < c, 1.0, 0.0).astype(BF16)

    @pl.when(grp == 0)
    def _():
        x = h_ref[...]
        xn = x * _rms_scale(x) * g_ref[...]
        xh = xn.astype(BF16)
        xn_sc[...] = xh
        xl = (xn - xh.astype(F32)).astype(BF16)
        logits = _dot(xh, wrh_ref[...]) + _dot(xl, wrh_ref[...]) + _dot(xh, wrl_ref[...]) + br_ref[...]
        lane = lax.broadcasted_iota(jnp.int32, (tm, LANES), 1)
        lane_f = lane.astype(F32)
        big = float(LANES)
        is_g = lane < N_GROUPS
        gl = jnp.where(is_g, logits, -jnp.inf)
        gmax = jnp.max(gl, axis=-1, keepdims=True)
        p_group = 1.0 / jnp.sum(jnp.where(is_g, jnp.exp(logits - gmax), 0.0), axis=-1, keepdims=True)
        gsel = jnp.min(jnp.where(gl == gmax, lane_f, big), axis=-1, keepdims=True)
        e_lane = lane - ROUTER_LANE0
        grp_of_lane = (e_lane // EXPERTS_PER_GROUP).astype(F32)
        in_grp = jnp.logical_and(jnp.logical_and(e_lane >= 0, e_lane < N_EXPERTS), grp_of_lane == gsel)
        el = jnp.where(in_grp, logits, -jnp.inf)
        emax = jnp.max(el, axis=-1, keepdims=True)
        ee = jnp.where(in_grp, jnp.exp(logits - emax), 0.0)
        prob = ee / jnp.sum(ee, axis=-1, keepdims=True)
        pm = jnp.where(in_grp, prob, -1.0)
        p1 = jnp.max(pm, axis=-1, keepdims=True)
        i1 = jnp.min(jnp.where(pm == p1, lane_f, big), axis=-1, keepdims=True)
        pm2 = jnp.where(lane_f == i1, -1.0, pm)
        p2 = jnp.max(pm2, axis=-1, keepdims=True)
        i2 = jnp.min(jnp.where(pm2 == p2, lane_f, big), axis=-1, keepdims=True)
        sel = jnp.logical_or(lane_f == i1, lane_f == i2)
        comb = jnp.where(sel, p_group * (prob / (p1 + p2)), 0.0)
        ch, cm, cl = _split3(comb)
        lane_sel = sel_ref[...]
        comb4 = _dot(ch, lane_sel) + _dot(cm, lane_sel) + _dot(cl, lane_sel)
        c4h, c4m, c4l = _split3(comb4)
        cw_sc[:, 0:LANES] = c4h
        cw_sc[:, LANES:2 * LANES] = c4m
        cw_sc[:, 2 * LANES:3 * LANES] = c4l
        onehot = jnp.where(lane_f == gsel, 1.0, 0.0)
        rank_col = jnp.sum(onehot * _dot(tril_sc[...], onehot.astype(BF16)), axis=-1, keepdims=True)
        mcol_sc[...] = jnp.where(lane == 0, gsel, jnp.where(lane == 1, rank_col, 0.0))
        cnt_sc[...] = jnp.broadcast_to(jnp.sum(onehot, axis=0, keepdims=True), cnt_sc.shape)
        oht = onehot.T
        cumt = _dot(oht.astype(BF16), triu_sc[...])
        sub = lax.broadcasted_iota(jnp.int32, (LANES, tm), 0)
        rank_row = jnp.sum(oht * cumt, axis=0, keepdims=True)
        grp_row = jnp.sum(oht * sub.astype(F32), axis=0, keepdims=True)
        srow = lax.broadcasted_iota(jnp.int32, mrow_sc.shape, 0)
        mrow_sc[...] = jnp.where(srow == 0, grp_row, jnp.where(srow == 1, rank_row, 0.0))
        acc_sc[...] = jnp.zeros(acc_sc.shape, F32)

    grp_f = grp.astype(F32)
    cnt_lane = lax.broadcasted_iota(jnp.int32, (1, LANES), 1)
    n_rows = jnp.sum(jnp.where(cnt_lane == grp, cnt_sc[0:1, :], 0.0)).astype(jnp.int32)
    rel_row = jnp.where(mrow_sc[0:1, :] == grp_f, mrow_sc[1:2, :], -1.0)
    mcol = mcol_sc[...]
    rel_col = jnp.where(mcol[:, 0:1] == grp_f, mcol[:, 1:2], -1.0)
    r_iota = lax.broadcasted_iota(jnp.int32, (MOE_SUB, tm), 0).astype(F32)
    c_iota = lax.broadcasted_iota(jnp.int32, (tm, MOE_SUB), 1).astype(F32)

    def sub_tile(s, carry):
        base = (s * MOE_SUB).astype(F32)
        gather = jnp.where(r_iota + base == rel_row, 1.0, 0.0).astype(BF16)
        xs = _dot(gather, xn_sc[...]).astype(BF16)
        cw3 = _dot(gather, cw_sc[...])
        cw = cw3[:, 0:LANES] + cw3[:, LANES:2 * LANES] + cw3[:, 2 * LANES:3 * LANES]
        y = jnp.zeros((MOE_SUB, out_ref.shape[1]), F32)
        for e in range(EXPERTS_PER_GROUP):
            hcat = _dot(xs, w13_ref[e])
            hg = hcat[:, :EXPERT_FF]
            hid = hg * _sigmoid(hg) * hcat[:, EXPERT_FF:]
            y = y + _dot((hid * cw[:, e:e + 1]).astype(BF16), w2_ref[e])
        scatter = jnp.where(c_iota + base == rel_col, 1.0, 0.0).astype(BF16)
        acc_sc[...] += _dot(scatter, y.astype(BF16))
        return carry

    lax.fori_loop(0, (n_rows + MOE_SUB - 1) // MOE_SUB, sub_tile, 0)

    @pl.when(grp == N_GROUPS - 1)
    def _():
        y = h_ref[...] + acc_sc[...]
        if final_norm:
            y = y * _rms_scale(y) * fg_ref[...]
        out_ref[...] = y


def _moe(h2d, ffn_g, w_group, b_group, w_router, b_router, w1, w3, w2, final_g, final_norm):
    t, d = h2d.shape
    tm = min(MOE_TM, t)
    pad = LANES - N_GROUPS - N_EXPERTS
    wr = jnp.concatenate([w_group, w_router, jnp.zeros((d, pad), F32)], axis=1).astype(F32)
    wr_hi = wr.astype(BF16)
    wr_lo = (wr - wr_hi.astype(F32)).astype(BF16)
    br = jnp.concatenate([b_group, b_router, jnp.zeros((pad,), F32)]).astype(F32).reshape(1, LANES)
    src = jnp.arange(LANES)[:, None]
    dst = jnp.arange(LANES)[None, :]
    is_expert = jnp.logical_and(src >= ROUTER_LANE0, src < ROUTER_LANE0 + N_EXPERTS)
    lane_sel = jnp.logical_and(is_expert, (src - ROUTER_LANE0) % EXPERTS_PER_GROUP == dst).astype(BF16)
    w13 = jnp.concatenate([w1, w3], axis=-1).astype(BF16)
    const = lambda i, g: (0, 0)
    return pl.pallas_call(
        functools.partial(_moe_kernel, final_norm=final_norm),
        grid=(t // tm, N_GROUPS),
        in_specs=[pl.BlockSpec((tm, d), lambda i, g: (i, 0)),
                  pl.BlockSpec((1, d), const),
                  pl.BlockSpec((d, LANES), const),
                  pl.BlockSpec((d, LANES), const),
                  pl.BlockSpec((1, LANES), const),
                  pl.BlockSpec((LANES, LANES), const),
                  pl.BlockSpec((EXPERTS_PER_GROUP, d, 2 * EXPERT_FF), lambda i, g: (g, 0, 0)),
                  pl.BlockSpec((EXPERTS_PER_GROUP, EXPERT_FF, d), lambda i, g: (g, 0, 0)),
                  pl.BlockSpec((1, d), const)],
        out_specs=pl.BlockSpec((tm, d), lambda i, g: (i, 0)),
        out_shape=jax.ShapeDtypeStruct((t, d), F32),
        scratch_shapes=[pltpu.VMEM((tm, d), BF16),
                        pltpu.VMEM((tm, 3 * LANES), BF16),
                        pltpu.VMEM((tm, LANES), F32),
                        pltpu.VMEM((8, tm), F32),
                        pltpu.VMEM((8, LANES), F32),
                        pltpu.VMEM((tm, d), F32),
                        pltpu.VMEM((tm, tm), BF16), pltpu.VMEM((tm, tm), BF16)],
        compiler_params=pltpu.CompilerParams(dimension_semantics=("arbitrary", "arbitrary"),
                                             vmem_limit_bytes=VMEM_LIMIT),
        name="moe",
    )(h2d, ffn_g.reshape(1, d), wr_hi, wr_lo, br, lane_sel, w13, w2.astype(BF16), final_g.reshape(1, d))


def kernel(x, mem, a_norm, a_w_in, a_conv, a_log, a_dt_bias, a_out_gain, a_w_out, kv_norm, w_kv, b_norm, b_w_in,
           b_w_out, mem_norm, w_mem_kv, ffn_norm, w_group, b_group, w_router, b_router, w1, w3, w2, final_norm):
    bsz, seq, d = x.shape
    m_len = mem.shape[1]
    depth = mem_norm.shape[0]
    n_a = a_norm.shape[0]
    h = x.reshape(bsz * seq, d)
    mem2d = mem.reshape(bsz * m_len, d)
    kv = None
    for l in range(depth):
        mkv = _norm_matmul(mem2d, mem_norm[l], w_mem_kv[l], F32, PROJ_TM)
        if l < n_a:
            w_in = a_w_in[l]
            g4 = 4 * GDN_WIDTH
            w_perm = jnp.concatenate([w_in[:, :g4], w_in[:, g4 + 2 * GDN_HEADS:], w_in[:, g4:g4 + 2 * GDN_HEADS],
                                      jnp.zeros((d, LANES - 2 * GDN_HEADS), w_in.dtype)], axis=1)
            proj = _norm_matmul(h, a_norm[l], w_perm, F32, PROJ_TM)
            o = _gdn(proj, a_conv[l], a_log[l], a_dt_bias[l], a_out_gain[l], bsz, seq)
            h = _mix_out(h, o, proj, g4 // MEM_WIDTH, mkv, a_w_out[l], bsz, seq, MIX_TS)
        else:
            lb = l - n_a
            if l == n_a:
                kv = _norm_matmul(h, kv_norm, w_kv, BF16, PROJ_TM)
            proj = _norm_matmul(h, b_norm[lb], b_w_in[lb], F32, PROJ_TM)
            o = _stick_breaking(proj, kv, bsz, seq)
            h = _mix_out(h, o, proj, SB_WIDTH // MEM_WIDTH, mkv, b_w_out[lb], bsz, seq, MIX_TS)
        h = _moe(h, ffn_norm[l], w_group[l], b_group[l], w_router[l], b_router[l], w1[l], w3[l], w2[l],
                 final_norm, l == depth - 1)
    return h.reshape(bsz, seq, d)
```

```python
import functools

import jax
import jax.numpy as jnp
from jax import lax
from jax.experimental import pallas as pl
from jax.experimental.pallas import tpu as pltpu

EPS = 1e-6
GDN_HEADS = 6
GDN_D = 128
GDN_WIDTH = GDN_HEADS * GDN_D
CONV_K = 4
CHUNK = 64
SB_HEADS = 12
SB_DH = 64
SB_WIDTH = SB_HEADS * SB_DH
SB_BLOCK = 128
MEM_HEADS = 4
MEM_DH = 64
MEM_WIDTH = MEM_HEADS * MEM_DH
N_GROUPS = 4
EXPERTS_PER_GROUP = 4
N_EXPERTS = N_GROUPS * EXPERTS_PER_GROUP
EXPERT_FF = 256
LANES = 128
A_IN_PAD = 4 * GDN_WIDTH + MEM_WIDTH + LANES
ROUTER_LANE0 = N_GROUPS
VMEM_LIMIT = 48 * 1024 * 1024
PROJ_TM = 512
MIX_TS = 256

F32 = jnp.float32
BF16 = jnp.bfloat16


def _dot(a, b):
    return jnp.dot(a, b, preferred_element_type=F32)


def _dot_nt(a, b):
    return lax.dot_general(a, b, (((1,), (1,)), ((), ())), preferred_element_type=F32)


def _dot_f32(a, b):
    return jnp.dot(a, b, preferred_element_type=F32, precision=lax.Precision.HIGHEST)


def _sigmoid(x):
    return 1.0 / (1.0 + jnp.exp(-x))


def _softplus(x):
    return jnp.maximum(x, 0.0) + jnp.log(1.0 + jnp.exp(-jnp.abs(x)))


def _rms_scale(x):
    return lax.rsqrt(jnp.mean(x * x, axis=-1, keepdims=True) + EPS)


def _norm_matmul_kernel(x_ref, g_ref, w_ref, o_ref, *, col_chunk):
    x = x_ref[...]
    xn = (x * _rms_scale(x) * g_ref[...]).astype(BF16)
    n = o_ref.shape[-1]
    for c0 in range(0, n, col_chunk):
        c1 = min(n, c0 + col_chunk)
        o_ref[:, c0:c1] = _dot(xn, w_ref[:, c0:c1]).astype(o_ref.dtype)


def _norm_matmul(x2d, g, w, out_dtype, tm):
    t, d = x2d.shape
    n = w.shape[1]
    tm = min(tm, t)
    return pl.pallas_call(
        functools.partial(_norm_matmul_kernel, col_chunk=4 * LANES),
        grid=(t // tm,),
        in_specs=[pl.BlockSpec((tm, d), lambda i: (i, 0)),
                  pl.BlockSpec((1, d), lambda i: (0, 0)),
                  pl.BlockSpec((d, n), lambda i: (0, 0))],
        out_specs=pl.BlockSpec((tm, n), lambda i: (i, 0)),
        out_shape=jax.ShapeDtypeStruct((t, n), out_dtype),
        compiler_params=pltpu.CompilerParams(dimension_semantics=("parallel",), vmem_limit_bytes=VMEM_LIMIT),
        name="norm_matmul",
    )(x2d, g.reshape(1, d), w.astype(BF16))


def _norm_matmul2_kernel(x_ref, ga_ref, wa_ref, gb_ref, wb_ref, oa_ref, ob_ref, *, col_chunk):
    x = x_ref[...]
    xs = x * _rms_scale(x)
    for g_ref, w_ref, o_ref in ((ga_ref, wa_ref, oa_ref), (gb_ref, wb_ref, ob_ref)):
        xn = (xs * g_ref[...]).astype(BF16)
        n = o_ref.shape[-1]
        for c0 in range(0, n, col_chunk):
            c1 = min(n, c0 + col_chunk)
            o_ref[:, c0:c1] = _dot(xn, w_ref[:, c0:c1]).astype(o_ref.dtype)


def _norm_matmul2(x2d, ga, wa, gb, wb, out_dtype, tm):
    t, d = x2d.shape
    na, nb = wa.shape[1], wb.shape[1]
    tm = min(tm, t)
    const = lambda i: (0, 0)
    return pl.pallas_call(
        functools.partial(_norm_matmul2_kernel, col_chunk=4 * LANES),
        grid=(t // tm,),
        in_specs=[pl.BlockSpec((tm, d), lambda i: (i, 0)),
                  pl.BlockSpec((1, d), const), pl.BlockSpec((d, na), const),
                  pl.BlockSpec((1, d), const), pl.BlockSpec((d, nb), const)],
        out_specs=[pl.BlockSpec((tm, na), lambda i: (i, 0)), pl.BlockSpec((tm, nb), lambda i: (i, 0))],
        out_shape=[jax.ShapeDtypeStruct((t, na), out_dtype), jax.ShapeDtypeStruct((t, nb), out_dtype)],
        compiler_params=pltpu.CompilerParams(dimension_semantics=("parallel",), vmem_limit_bytes=VMEM_LIMIT),
        name="norm_matmul2",
    )(x2d, ga.reshape(1, d), wa.astype(BF16), gb.reshape(1, d), wb.astype(BF16))


GDN_STEP_CHUNKS = 4
GDN_TOK = CHUNK * GDN_STEP_CHUNKS
CONV_TAIL = 8


def _gdn_kernel(qkv_ref, gate_ref, ba_ref, convw_ref, alog_ref, dtb_ref, ogain_ref, o_ref, xbuf, state):
    c = CHUNK
    tok = GDN_TOK

    @pl.when(pl.program_id(1) == 0)
    def _():
        xbuf[0:CONV_TAIL, :] = jnp.zeros((CONV_TAIL, xbuf.shape[1]), F32)
        state[...] = jnp.zeros(state.shape, F32)

    xbuf[CONV_TAIL:CONV_TAIL + tok, :] = qkv_ref[...]

    def conv_act(lo):
        win = xbuf[:, lo:lo + GDN_D]
        acc = win[CONV_TAIL:] * convw_ref[CONV_K - 1:CONV_K, lo:lo + GDN_D]
        for j in range(CONV_K - 1):
            shifted = pltpu.roll(win, CONV_K - 1 - j, axis=0)[CONV_TAIL:]
            acc = acc + shifted * convw_ref[j:j + 1, lo:lo + GDN_D]
        return acc * _sigmoid(acc)

    ba = ba_ref[...]
    beta_all = _sigmoid(ba)
    g_all = -jnp.exp(alog_ref[...]) * _softplus(ba + dtb_ref[...])
    trow = lax.broadcasted_iota(jnp.int32, (tok, tok), 0)
    tcol = lax.broadcasted_iota(jnp.int32, (tok, tok), 1)
    same_chunk = (trow // c) == (tcol // c)
    lower = jnp.where(jnp.logical_and(same_chunk, trow >= tcol), 1.0, 0.0)
    upper = jnp.where(jnp.logical_and(same_chunk, trow <= tcol), 1.0, 0.0)
    gc_all = _dot_f32(lower, g_all)
    gct_all = _dot_f32(g_all.T, upper)
    row = lax.broadcasted_iota(jnp.int32, (c, c), 0)
    col = lax.broadcasted_iota(jnp.int32, (c, c), 1)
    incl = row >= col
    strict = row > col
    eye = (row == col).astype(F32)
    ogain = ogain_ref[...]

    heads = range(GDN_HEADS)
    chunks = range(GDN_STEP_CHUNKS)
    items = [(ci, h) for ci in chunks for h in heads]

    q_full, k_full, v_full = [], [], []
    for h in heads:
        lo = h * GDN_D
        qf = conv_act(lo)
        kf = conv_act(GDN_WIDTH + lo)
        q_full.append(qf * lax.rsqrt(jnp.sum(qf * qf, axis=-1, keepdims=True) + EPS) * (GDN_D ** -0.5))
        k_full.append(kf * lax.rsqrt(jnp.sum(kf * kf, axis=-1, keepdims=True) + EPS))
        v_full.append(conv_act(2 * GDN_WIDTH + lo))

    pre = {}
    for ci, h in items:
        r0 = ci * c
        gl = GDN_HEADS + h
        qh = q_full[h][r0:r0 + c]
        kh = k_full[h][r0:r0 + c]
        vh = v_full[h][r0:r0 + c]
        beta = beta_all[r0:r0 + c, h:h + 1]
        gcol = gc_all[r0:r0 + c, gl:gl + 1]
        grow = gct_all[gl:gl + 1, r0:r0 + c]
        glast = gc_all[r0 + c - 1:r0 + c, gl:gl + 1]
        decay = jnp.where(incl, jnp.exp(jnp.where(incl, gcol - grow, 0.0)), 0.0)
        kb = kh * beta
        khb = kh.astype(BF16)
        egc = jnp.exp(gcol)
        pre[ci, h] = dict(
            p=-jnp.where(strict, _dot_nt(kb.astype(BF16), khb) * decay, 0.0),
            att=jnp.where(incl, _dot_nt(qh.astype(BF16), khb) * decay, 0.0).astype(BF16),
            rhs=jnp.concatenate([vh * beta, kb * egc], axis=1).astype(BF16),
            qg=qh * egc,
            ktail_t=(kh * jnp.exp(glast - gcol)).T.astype(BF16),
            sdecay=jnp.exp(glast))

    tinv = {it: eye + pre[it]["p"] for it in items}
    pw = {it: pre[it]["p"] for it in items}
    for _ in range(5):
        for it in items:
            pb = pw[it].astype(BF16)
            pw[it] = _dot(pb, pb)
        for it in items:
            tinv[it] = tinv[it] + _dot(tinv[it].astype(BF16), pw[it].astype(BF16))
    sol = {it: _dot(tinv[it].astype(BF16), pre[it]["rhs"]) for it in items}

    drow = lax.broadcasted_iota(jnp.int32, (GDN_D, GDN_D), 0)
    dcol = lax.broadcasted_iota(jnp.int32, (GDN_D, GDN_D), 1)
    eye_d = (drow == dcol).astype(F32)
    lin = {}
    for it in items:
        pr = pre[it]
        solb = sol[it].astype(BF16)
        att_sol = _dot(pr["att"], solb)
        kt_sol = _dot(pr["ktail_t"], solb)
        lin[it] = ((pr["qg"] - att_sol[:, GDN_D:]).astype(BF16), att_sol[:, :GDN_D],
                   (eye_d * pr["sdecay"] - kt_sol[:, GDN_D:]).astype(BF16), kt_sol[:, :GDN_D])
    s_cur = [state[h] for h in heads]
    for ci in chunks:
        r0 = ci * c
        for h in heads:
            q_lin, o_const, s_lin, s_const = lin[ci, h]
            lo = h * GDN_D
            sb = s_cur[h].astype(BF16)
            o = _dot(q_lin, sb) + o_const
            s_cur[h] = _dot(s_lin, sb) + s_const
            gate = gate_ref[r0:r0 + c, lo:lo + GDN_D]
            o = o * _rms_scale(o) * ogain * (gate * _sigmoid(gate))
            o_ref[r0:r0 + c, lo:lo + GDN_D] = o.astype(o_ref.dtype)
    for h in heads:
        state[h] = s_cur[h]
    xbuf[0:CONV_TAIL, :] = xbuf[tok:tok + CONV_TAIL, :]


def _gdn(proj, conv_w, a_log, dt_bias, o_gain, bsz, seq):
    t = proj.shape[0]
    n_s = seq // GDN_TOK
    qkv_w = 3 * GDN_WIDTH
    pad = jnp.zeros((GDN_HEADS,), F32)
    alog_row = jnp.concatenate([pad, a_log.astype(F32), jnp.zeros((LANES - 2 * GDN_HEADS,), F32)]).reshape(1, LANES)
    dtb_row = jnp.concatenate([pad, dt_bias.astype(F32), jnp.zeros((LANES - 2 * GDN_HEADS,), F32)]).reshape(1, LANES)
    return pl.pallas_call(
        _gdn_kernel,
        grid=(bsz, n_s),
        in_specs=[pl.BlockSpec((GDN_TOK, qkv_w), lambda b, s: (b * n_s + s, 0)),
                  pl.BlockSpec((GDN_TOK, GDN_WIDTH), lambda b, s: (b * n_s + s, qkv_w // GDN_WIDTH)),
                  pl.BlockSpec((GDN_TOK, LANES), lambda b, s: (b * n_s + s, (A_IN_PAD - LANES) // LANES)),
                  pl.BlockSpec((CONV_K, qkv_w), lambda b, s: (0, 0)),
                  pl.BlockSpec((1, LANES), lambda b, s: (0, 0)),
                  pl.BlockSpec((1, LANES), lambda b, s: (0, 0)),
                  pl.BlockSpec((1, GDN_D), lambda b, s: (0, 0))],
        out_specs=pl.BlockSpec((GDN_TOK, GDN_WIDTH), lambda b, s: (b * n_s + s, 0)),
        out_shape=jax.ShapeDtypeStruct((t, GDN_WIDTH), BF16),
        scratch_shapes=[pltpu.VMEM((GDN_TOK + CONV_TAIL, qkv_w), F32),
                        pltpu.VMEM((GDN_HEADS, GDN_D, GDN_D), F32)],
        compiler_params=pltpu.CompilerParams(dimension_semantics=("parallel", "arbitrary"),
                                             vmem_limit_bytes=VMEM_LIMIT),
        name="gdn",
    )(proj, proj, proj, conv_w.astype(F32), alog_row, dtb_row, o_gain.astype(F32).reshape(1, GDN_D))


def _mix_out_kernel(h_ref, o_ref, mq_ref, mk_ref, mv_ref, wo_ref, wm_ref, out_ref):
    mq = (mq_ref[...] * (MEM_DH ** -0.5)).astype(BF16)
    mk = mk_ref[...]
    mv = mv_ref[...]
    head = lax.broadcasted_iota(jnp.int32, mk.shape, 1) // MEM_DH
    heads = range(MEM_HEADS)
    k_heads = [jnp.where(head == hh, mk, 0.0).astype(BF16) for hh in heads]
    v_heads = [jnp.where(head == hh, mv, 0.0).astype(BF16) for hh in heads]
    scores = [_dot_nt(mq, k_heads[hh]) for hh in heads]
    probs = []
    for sc in scores:
        p = jnp.exp(sc - jnp.max(sc, axis=-1, keepdims=True))
        probs.append((p * (1.0 / jnp.sum(p, axis=-1, keepdims=True))).astype(BF16))
    m = _dot(probs[0], v_heads[0])
    for hh in heads[1:]:
        m = m + _dot(probs[hh], v_heads[hh])
    y = _dot(o_ref[...].astype(BF16), wo_ref[...]) + _dot(m.astype(BF16), wm_ref[...])
    out_ref[...] = h_ref[...] + y


def _mix_out(h2d, o, proj, mq_block, mkv, w_out, bsz, seq, ts):
    t, d = h2d.shape
    ts = min(ts, seq)
    n_s = seq // ts
    m_len = mkv.shape[0] // bsz
    ow = o.shape[1]
    return pl.pallas_call(
        _mix_out_kernel,
        grid=(bsz, n_s),
        in_specs=[pl.BlockSpec((ts, d), lambda b, s: (b * n_s + s, 0)),
                  pl.BlockSpec((ts, ow), lambda b, s: (b * n_s + s, 0)),
                  pl.BlockSpec((ts, MEM_WIDTH), lambda b, s: (b * n_s + s, mq_block)),
                  pl.BlockSpec((m_len, MEM_WIDTH), lambda b, s: (b, 0)),
                  pl.BlockSpec((m_len, MEM_WIDTH), lambda b, s: (b, 1)),
                  pl.BlockSpec((ow, d), lambda b, s: (0, 0)),
                  pl.BlockSpec((MEM_WIDTH, d), lambda b, s: (0, 0))],
        out_specs=pl.BlockSpec((ts, d), lambda b, s: (b * n_s + s, 0)),
        out_shape=jax.ShapeDtypeStruct((t, d), F32),
        compiler_params=pltpu.CompilerParams(dimension_semantics=("parallel", "parallel"),
                                             vmem_limit_bytes=VMEM_LIMIT),
        name="mix_out",
    )(h2d, o, proj, mkv, mkv, w_out[:ow].astype(BF16), w_out[ow:].astype(BF16))


SB_SUB = 8
SB_NEG = -1e30
SB_DONE = 88.0


def _split_bf16(x):
    hi = pltpu.bitcast(pltpu.bitcast(x, jnp.uint32) & jnp.uint32(0xFFFF0000), F32)
    return hi.astype(BF16), (x - hi).astype(BF16)


def _sb_kernel(q_ref, k_ref, v_ref, o_ref):
    blk = SB_BLOCK
    step = pl.program_id(2)
    lane = lax.broadcasted_iota(jnp.int32, (blk, 2 * SB_DH), 1)
    row1 = lax.broadcasted_iota(jnp.int32, (blk, blk), 0)
    col1 = lax.broadcasted_iota(jnp.int32, (blk, blk), 1)
    causal = col1 < row1
    srow2 = lax.broadcasted_iota(jnp.int32, (2 * blk, 2 * blk), 0)
    scol2 = lax.broadcasted_iota(jnp.int32, (2 * blk, 2 * blk), 1)
    suffix2 = (srow2 >= scol2).astype(BF16)
    suffix1 = suffix2[:blk, :blk]
    suffix2x2 = jnp.concatenate([suffix2, suffix2], axis=0)
    suffix1x2 = jnp.concatenate([suffix1, suffix1], axis=0)

    def suffix_sums(sp, suffix_x2):
        return _dot(jnp.concatenate(_split_bf16(sp), axis=1), suffix_x2)

    q_heads, k2s, v2s = [], [], []
    for j in range(SB_SUB):
        i = step * SB_SUB + j
        q = q_ref[j * blk:(j + 1) * blk, :] * (SB_DH ** -0.5)
        q_heads.append(jnp.where(lane < SB_DH, q, 0.0).astype(BF16))
        q_heads.append(jnp.where(lane >= SB_DH, q, 0.0).astype(BF16))
        prev = pl.multiple_of(jnp.maximum(i - 1, 0) * blk, blk)
        diag = pl.multiple_of(i * blk, blk)
        k2s.append(jnp.concatenate([k_ref[pl.ds(prev, blk), :], k_ref[pl.ds(diag, blk), :]], axis=0))
        has_prev = jnp.where(i > 0, 1.0, 0.0).astype(BF16)
        v2s.append(jnp.concatenate([v_ref[pl.ds(prev, blk), :] * has_prev, v_ref[pl.ds(diag, blk), :]], axis=0))
    chains = range(2 * SB_SUB)
    raw = [_dot_nt(q_heads[n], k2s[n // 2]) for n in chains]
    zs = [jnp.concatenate([raw[n][:, :blk], jnp.where(causal, raw[n][:, blk:], SB_NEG)], axis=1) for n in chains]
    sums = [suffix_sums(_softplus(zs[n]), suffix2x2) for n in chains]
    probs = [jnp.exp(zs[n] - sums[n]).astype(BF16) for n in chains]
    carries = [sums[n][:, 0:1] for n in chains]
    accs = [_dot(probs[n], v2s[n // 2]) for n in chains]

    def lowest(cs):
        m = cs[0]
        for cc in cs[1:]:
            m = jnp.minimum(m, cc)
        return jnp.min(m)

    def more(st):
        it, cmin = st[0], st[1]
        return jnp.logical_and(it < (step + 1) * SB_SUB - 2, cmin < SB_DONE)

    def body(st):
        it = st[0]
        cs = list(st[2:2 + 2 * SB_SUB])
        acs = list(st[2 + 2 * SB_SUB:])
        kblks, vblks = [], []
        for j in range(SB_SUB):
            kb = step * SB_SUB + j - 2 - it
            start = pl.multiple_of(jnp.maximum(kb, 0) * blk, blk)
            kblks.append(k_ref[pl.ds(start, blk), :])
            vblks.append(v_ref[pl.ds(start, blk), :] * jnp.where(kb >= 0, 1.0, 0.0).astype(BF16))
        zl = [_dot_nt(q_heads[n], kblks[n // 2]) for n in chains]
        sl = [suffix_sums(_softplus(zl[n]), suffix1x2) for n in chains]
        pl_ = [jnp.exp(zl[n] - cs[n] - sl[n]).astype(BF16) for n in chains]
        acs = [acs[n] + _dot(pl_[n], vblks[n // 2]) for n in chains]
        cs = [cs[n] + sl[n][:, 0:1] for n in chains]
        return (it + 1, lowest(cs), *cs, *acs)

    st = lax.while_loop(more, body, (jnp.int32(0), lowest(carries), *carries, *accs))
    accs = st[2 + 2 * SB_SUB:]
    for j in range(SB_SUB):
        o_ref[j * blk:(j + 1) * blk, :] = jnp.where(lane < SB_DH, accs[2 * j], accs[2 * j + 1]).astype(o_ref.dtype)


def _stick_breaking(qproj, kv, bsz, seq):
    t = qproj.shape[0]
    tq = SB_SUB * SB_BLOCK
    n_q = seq // tq
    pairs = SB_HEADS // 2
    pw = 2 * SB_DH
    return pl.pallas_call(
        _sb_kernel,
        grid=(bsz, pairs, n_q),
        in_specs=[pl.BlockSpec((tq, pw), lambda b, p, i: (b * n_q + i, p)),
                  pl.BlockSpec((seq, pw), lambda b, p, i: (b, p)),
                  pl.BlockSpec((seq, pw), lambda b, p, i: (b, pairs + p))],
        out_specs=pl.BlockSpec((tq, pw), lambda b, p, i: (b * n_q + i, p)),
        out_shape=jax.ShapeDtypeStruct((t, SB_WIDTH), BF16),
        compiler_params=pltpu.CompilerParams(dimension_semantics=("parallel", "parallel", "arbitrary"),
                                             vmem_limit_bytes=VMEM_LIMIT),
        name="stick_breaking",
    )(qproj, kv, kv)


MOE_TM = 1024
MOE_SUB = 320


def _split3(x):
    hi = x.astype(BF16)
    r1 = x - hi.astype(F32)
    mid = r1.astype(BF16)
    return hi, mid, (r1 - mid.astype(F32)).astype(BF16)


def _moe_kernel(h_ref, g_ref, wrh_ref, wrl_ref, br_ref, sel_ref, w13_ref, w2_ref, fg_ref, out_ref,
                xn_sc, cw_sc, mcol_sc, mrow_sc, cnt_sc, acc_sc, tril_sc, triu_sc, *, final_norm):
    tile = pl.program_id(0)
    grp = pl.program_id(1)
    tm = h_ref.shape[0]

    @pl.when(jnp.logical_and(tile == 0, grp == 0))
    def _():
        r = lax.broadcasted_iota(jnp.int32, (tm, tm), 0)
        c = lax.broadcasted_iota(jnp.int32, (tm, tm), 1)
        tril_sc[...] = jnp.where(r > c, 1.0, 0.0).astype(BF16)
        triu_sc[...] = jnp.where(r < c, 1.0, 0.0).astype(BF16)

    @pl.when(grp == 0)
    def _():
        x = h_ref[...]
        xn = x * _rms_scale(x) * g_ref[...]
        xh = xn.astype(BF16)
        xn_sc[...] = xh
        xl = (xn - xh.astype(F32)).astype(BF16)
        logits = _dot(xh, wrh_ref[...]) + _dot(xl, wrh_ref[...]) + _dot(xh, wrl_ref[...]) + br_ref[...]
        lane = lax.broadcasted_iota(jnp.int32, (tm, LANES), 1)
        lane_f = lane.astype(F32)
        big = float(LANES)
        is_g = lane < N_GROUPS
        gl = jnp.where(is_g, logits, -jnp.inf)
        gmax = jnp.max(gl, axis=-1, keepdims=True)
        p_group = 1.0 / jnp.sum(jnp.where(is_g, jnp.exp(logits - gmax), 0.0), axis=-1, keepdims=True)
        gsel = jnp.min(jnp.where(gl == gmax, lane_f, big), axis=-1, keepdims=True)
        e_lane = lane - ROUTER_LANE0
        grp_of_lane = (e_lane // EXPERTS_PER_GROUP).astype(F32)
        in_grp = jnp.logical_and(jnp.logical_and(e_lane >= 0, e_lane < N_EXPERTS), grp_of_lane == gsel)
        el = jnp.where(in_grp, logits, -jnp.inf)
        emax = jnp.max(el, axis=-1, keepdims=True)
        ee = jnp.where(in_grp, jnp.exp(logits - emax), 0.0)
        prob = ee / jnp.sum(ee, axis=-1, keepdims=True)
        pm = jnp.where(in_grp, prob, -1.0)
        p1 = jnp.max(pm, axis=-1, keepdims=True)
        i1 = jnp.min(jnp.where(pm == p1, lane_f, big), axis=-1, keepdims=True)
        pm2 = jnp.where(lane_f == i1, -1.0, pm)
        p2 = jnp.max(pm2, axis=-1, keepdims=True)
        i2 = jnp.min(jnp.where(pm2 == p2, lane_f, big), axis=-1, keepdims=True)
        sel = jnp.logical_or(lane_f == i1, lane_f == i2)
        comb = jnp.where(sel, p_group * (prob / (p1 + p2)), 0.0)
        ch, cm, cl = _split3(comb)
        lane_sel = sel_ref[...]
        comb4 = _dot(ch, lane_sel) + _dot(cm, lane_sel) + _dot(cl, lane_sel)
        c4h, c4m, c4l = _split3(comb4)
        cw_sc[:, 0:LANES] = c4h
        cw_sc[:, LANES:2 * LANES] = c4m
        cw_sc[:, 2 * LANES:3 * LANES] = c4l
        onehot = jnp.where(lane_f == gsel, 1.0, 0.0)
        rank_col = jnp.sum(onehot * _dot(tril_sc[...], onehot.astype(BF16)), axis=-1, keepdims=True)
        mcol_sc[...] = jnp.where(lane == 0, gsel, jnp.where(lane == 1, rank_col, 0.0))
        cnt_sc[...] = jnp.broadcast_to(jnp.sum(onehot, axis=0, keepdims=True), cnt_sc.shape)
        oht = onehot.T
        cumt = _dot(oht.astype(BF16), triu_sc[...])
        sub = lax.broadcasted_iota(jnp.int32, (LANES, tm), 0)
        rank_row = jnp.sum(oht * cumt, axis=0, keepdims=True)
        grp_row = jnp.sum(oht * sub.astype(F32), axis=0, keepdims=True)
        srow = lax.broadcasted_iota(jnp.int32, mrow_sc.shape, 0)
        mrow_sc[...] = jnp.where(srow == 0, grp_row, jnp.where(srow == 1, rank_row, 0.0))
        acc_sc[...] = jnp.zeros(acc_sc.shape, F32)

    grp_f = grp.astype(F32)
    cnt_lane = lax.broadcasted_iota(jnp.int32, (1, LANES), 1)
    n_rows = jnp.sum(jnp.where(cnt_lane == grp, cnt_sc[0:1, :], 0.0)).astype(jnp.int32)
    rel_row = jnp.where(mrow_sc[0:1, :] == grp_f, mrow_sc[1:2, :], -1.0)
    mcol = mcol_sc[...]
    rel_col = jnp.where(mcol[:, 0:1] == grp_f, mcol[:, 1:2], -1.0)
    r_iota = lax.broadcasted_iota(jnp.int32, (MOE_SUB, tm), 0).astype(F32)
    c_iota = lax.broadcasted_iota(jnp.int32, (tm, MOE_SUB), 1).astype(F32)

    def sub_tile(s, carry):
        base = (s * MOE_SUB).astype(F32)
        gather = jnp.where(r_iota + base == rel_row, 1.0, 0.0).astype(BF16)
        xs = _dot(gather, xn_sc[...]).astype(BF16)
        cw3 = _dot(gather, cw_sc[...])
        cw = cw3[:, 0:LANES] + cw3[:, LANES:2 * LANES] + cw3[:, 2 * LANES:3 * LANES]
        y = jnp.zeros((MOE_SUB, out_ref.shape[1]), F32)
        for e in range(EXPERTS_PER_GROUP):
            hcat = _dot(xs, w13_ref[e])
            hg = hcat[:, :EXPERT_FF]
            hid = hg * _sigmoid(hg) * hcat[:, EXPERT_FF:]
            y = y + _dot((hid * cw[:, e:e + 1]).astype(BF16), w2_ref[e])
        scatter = jnp.where(c_iota + base == rel_col, 1.0, 0.0).astype(BF16)
        acc_sc[...] += _dot(scatter, y.astype(BF16))
        return carry

    lax.fori_loop(0, (n_rows + MOE_SUB - 1) // MOE_SUB, sub_tile, 0)

    @pl.when(grp == N_GROUPS - 1)
    def _():
        y = h_ref[...] + acc_sc[...]
        if final_norm:
            y = y * _rms_scale(y) * fg_ref[...]
        out_ref[...] = y


def _moe(h2d, ffn_g, w_group, b_group, w_router, b_router, w1, w3, w2, final_g, final_norm):
    t, d = h2d.shape
    tm = min(MOE_TM, t)
    pad = LANES - N_GROUPS - N_EXPERTS
    wr = jnp.concatenate([w_group, w_router, jnp.zeros((d, pad), F32)], axis=1).astype(F32)
    wr_hi = wr.astype(BF16)
    wr_lo = (wr - wr_hi.astype(F32)).astype(BF16)
    br = jnp.concatenate([b_group, b_router, jnp.zeros((pad,), F32)]).astype(F32).reshape(1, LANES)
    src = jnp.arange(LANES)[:, None]
    dst = jnp.arange(LANES)[None, :]
    is_expert = jnp.logical_and(src >= ROUTER_LANE0, src < ROUTER_LANE0 + N_EXPERTS)
    lane_sel = jnp.logical_and(is_expert, (src - ROUTER_LANE0) % EXPERTS_PER_GROUP == dst).astype(BF16)
    w13 = jnp.concatenate([w1, w3], axis=-1).astype(BF16)
    const = lambda i, g: (0, 0)
    return pl.pallas_call(
        functools.partial(_moe_kernel, final_norm=final_norm),
        grid=(t // tm, N_GROUPS),
        in_specs=[pl.BlockSpec((tm, d), lambda i, g: (i, 0)),
                  pl.BlockSpec((1, d), const),
                  pl.BlockSpec((d, LANES), const),
                  pl.BlockSpec((d, LANES), const),
                  pl.BlockSpec((1, LANES), const),
                  pl.BlockSpec((LANES, LANES), const),
                  pl.BlockSpec((EXPERTS_PER_GROUP, d, 2 * EXPERT_FF), lambda i, g: (g, 0, 0)),
                  pl.BlockSpec((EXPERTS_PER_GROUP, EXPERT_FF, d), lambda i, g: (g, 0, 0)),
                  pl.BlockSpec((1, d), const)],
        out_specs=pl.BlockSpec((tm, d), lambda i, g: (i, 0)),
        out_shape=jax.ShapeDtypeStruct((t, d), F32),
        scratch_shapes=[pltpu.VMEM((tm, d), BF16),
                        pltpu.VMEM((tm, 3 * LANES), BF16),
                        pltpu.VMEM((tm, LANES), F32),
                        pltpu.VMEM((8, tm), F32),
                        pltpu.VMEM((8, LANES), F32),
                        pltpu.VMEM((tm, d), F32),
                        pltpu.VMEM((tm, tm), BF16), pltpu.VMEM((tm, tm), BF16)],
        compiler_params=pltpu.CompilerParams(dimension_semantics=("arbitrary", "arbitrary"),
                                             vmem_limit_bytes=VMEM_LIMIT),
        name="moe",
    )(h2d, ffn_g.reshape(1, d), wr_hi, wr_lo, br, lane_sel, w13, w2.astype(BF16), final_g.reshape(1, d))


def kernel(x, mem, a_norm, a_w_in, a_conv, a_log, a_dt_bias, a_out_gain, a_w_out, kv_norm, w_kv, b_norm, b_w_in,
           b_w_out, mem_norm, w_mem_kv, ffn_norm, w_group, b_group, w_router, b_router, w1, w3, w2, final_norm):
    bsz, seq, d = x.shape
    assert seq % (SB_SUB * SB_BLOCK) == 0 and seq % GDN_TOK == 0 and seq % MIX_TS == 0, seq
    m_len = mem.shape[1]
    depth = mem_norm.shape[0]
    n_a = a_norm.shape[0]
    h = x.reshape(bsz * seq, d)
    mem2d = mem.reshape(bsz * m_len, d)
    kv = None
    for l in range(depth):
        mkv = _norm_matmul(mem2d, mem_norm[l], w_mem_kv[l], F32, PROJ_TM)
        if l < n_a:
            w_in = a_w_in[l]
            g4 = 4 * GDN_WIDTH
            w_perm = jnp.concatenate([w_in[:, :g4], w_in[:, g4 + 2 * GDN_HEADS:], w_in[:, g4:g4 + 2 * GDN_HEADS],
                                      jnp.zeros((d, LANES - 2 * GDN_HEADS), w_in.dtype)], axis=1)
            proj = _norm_matmul(h, a_norm[l], w_perm, F32, PROJ_TM)
            o = _gdn(proj, a_conv[l], a_log[l], a_dt_bias[l], a_out_gain[l], bsz, seq)
            h = _mix_out(h, o, proj, g4 // MEM_WIDTH, mkv, a_w_out[l], bsz, seq, MIX_TS)
        else:
            lb = l - n_a
            if l == n_a:
                kv, proj = _norm_matmul2(h, kv_norm, w_kv, b_norm[lb], b_w_in[lb], BF16, PROJ_TM)
            else:
                proj = _norm_matmul(h, b_norm[lb], b_w_in[lb], BF16, PROJ_TM)
            o = _stick_breaking(proj, kv, bsz, seq)
            h = _mix_out(h, o, proj, SB_WIDTH // MEM_WIDTH, mkv, b_w_out[lb], bsz, seq, MIX_TS)
        h = _moe(h, ffn_norm[l], w_group[l], b_group[l], w_router[l], b_router[l], w1[l], w3[l], w2[l],
                 final_norm, l == depth - 1)
    return h.reshape(bsz, seq, d)
```

```python
import functools

import jax
import jax.numpy as jnp
from jax import lax
from jax.experimental import pallas as pl
from jax.experimental.pallas import tpu as pltpu

EPS = 1e-6
GDN_HEADS = 6
GDN_D = 128
GDN_WIDTH = GDN_HEADS * GDN_D
CONV_K = 4
CHUNK = 64
SB_HEADS = 12
SB_DH = 64
SB_WIDTH = SB_HEADS * SB_DH
SB_BLOCK = 128
MEM_HEADS = 4
MEM_DH = 64
MEM_WIDTH = MEM_HEADS * MEM_DH
N_GROUPS = 4
EXPERTS_PER_GROUP = 4
N_EXPERTS = N_GROUPS * EXPERTS_PER_GROUP
EXPERT_FF = 256
LANES = 128
A_IN_PAD = 4 * GDN_WIDTH + MEM_WIDTH + LANES
ROUTER_LANE0 = N_GROUPS
VMEM_LIMIT = 48 * 1024 * 1024
PROJ_TM = 512
MIX_TS = 256

F32 = jnp.float32
BF16 = jnp.bfloat16


def _dot(a, b):
    return jnp.dot(a, b, preferred_element_type=F32)


def _dot_nt(a, b):
    return lax.dot_general(a, b, (((1,), (1,)), ((), ())), preferred_element_type=F32)


def _dot_f32(a, b):
    return jnp.dot(a, b, preferred_element_type=F32, precision=lax.Precision.HIGHEST)


def _sigmoid(x):
    return 1.0 / (1.0 + jnp.exp(-x))


def _softplus(x):
    return jnp.maximum(x, 0.0) + jnp.log(1.0 + jnp.exp(-jnp.abs(x)))


def _rms_scale(x):
    return lax.rsqrt(jnp.mean(x * x, axis=-1, keepdims=True) + EPS)


def _norm_matmul_kernel(x_ref, g_ref, w_ref, o_ref, *, col_chunk):
    x = x_ref[...]
    xn = (x * _rms_scale(x) * g_ref[...]).astype(BF16)
    n = o_ref.shape[-1]
    for c0 in range(0, n, col_chunk):
        c1 = min(n, c0 + col_chunk)
        o_ref[:, c0:c1] = _dot(xn, w_ref[:, c0:c1]).astype(o_ref.dtype)


def _norm_matmul(x2d, g, w, out_dtype, tm):
    t, d = x2d.shape
    n = w.shape[1]
    tm = min(tm, t)
    return pl.pallas_call(
        functools.partial(_norm_matmul_kernel, col_chunk=4 * LANES),
        grid=(t // tm,),
        in_specs=[pl.BlockSpec((tm, d), lambda i: (i, 0)),
                  pl.BlockSpec((1, d), lambda i: (0, 0)),
                  pl.BlockSpec((d, n), lambda i: (0, 0))],
        out_specs=pl.BlockSpec((tm, n), lambda i: (i, 0)),
        out_shape=jax.ShapeDtypeStruct((t, n), out_dtype),
        compiler_params=pltpu.CompilerParams(dimension_semantics=("parallel",), vmem_limit_bytes=VMEM_LIMIT),
        name="norm_matmul",
    )(x2d, g.reshape(1, d), w.astype(BF16))


def _norm_matmul2_kernel(x_ref, ga_ref, wa_ref, gb_ref, wb_ref, oa_ref, ob_ref, *, col_chunk):
    x = x_ref[...]
    xs = x * _rms_scale(x)
    for g_ref, w_ref, o_ref in ((ga_ref, wa_ref, oa_ref), (gb_ref, wb_ref, ob_ref)):
        xn = (xs * g_ref[...]).astype(BF16)
        n = o_ref.shape[-1]
        for c0 in range(0, n, col_chunk):
            c1 = min(n, c0 + col_chunk)
            o_ref[:, c0:c1] = _dot(xn, w_ref[:, c0:c1]).astype(o_ref.dtype)


def _norm_matmul2(x2d, ga, wa, gb, wb, out_dtype, tm):
    t, d = x2d.shape
    na, nb = wa.shape[1], wb.shape[1]
    tm = min(tm, t)
    const = lambda i: (0, 0)
    return pl.pallas_call(
        functools.partial(_norm_matmul2_kernel, col_chunk=4 * LANES),
        grid=(t // tm,),
        in_specs=[pl.BlockSpec((tm, d), lambda i: (i, 0)),
                  pl.BlockSpec((1, d), const), pl.BlockSpec((d, na), const),
                  pl.BlockSpec((1, d), const), pl.BlockSpec((d, nb), const)],
        out_specs=[pl.BlockSpec((tm, na), lambda i: (i, 0)), pl.BlockSpec((tm, nb), lambda i: (i, 0))],
        out_shape=[jax.ShapeDtypeStruct((t, na), out_dtype), jax.ShapeDtypeStruct((t, nb), out_dtype)],
        compiler_params=pltpu.CompilerParams(dimension_semantics=("parallel",), vmem_limit_bytes=VMEM_LIMIT),
        name="norm_matmul2",
    )(x2d, ga.reshape(1, d), wa.astype(BF16), gb.reshape(1, d), wb.astype(BF16))


GDN_STEP_CHUNKS = 4
GDN_TOK = CHUNK * GDN_STEP_CHUNKS
CONV_TAIL = 8


def _gdn_kernel(qkv_ref, gate_ref, ba_ref, convw_ref, alog_ref, dtb_ref, ogain_ref, o_ref, xbuf, state):
    c = CHUNK
    tok = GDN_TOK

    @pl.when(pl.program_id(1) == 0)
    def _():
        xbuf[0:CONV_TAIL, :] = jnp.zeros((CONV_TAIL, xbuf.shape[1]), F32)
        state[...] = jnp.zeros(state.shape, F32)

    xbuf[CONV_TAIL:CONV_TAIL + tok, :] = qkv_ref[...]

    def conv_act(lo):
        win = xbuf[:, lo:lo + GDN_D]
        acc = win[CONV_TAIL:] * convw_ref[CONV_K - 1:CONV_K, lo:lo + GDN_D]
        for j in range(CONV_K - 1):
            shifted = pltpu.roll(win, CONV_K - 1 - j, axis=0)[CONV_TAIL:]
            acc = acc + shifted * convw_ref[j:j + 1, lo:lo + GDN_D]
        return acc * _sigmoid(acc)

    ba = ba_ref[...]
    beta_all = _sigmoid(ba)
    g_all = -jnp.exp(alog_ref[...]) * _softplus(ba + dtb_ref[...])
    trow = lax.broadcasted_iota(jnp.int32, (tok, tok), 0)
    tcol = lax.broadcasted_iota(jnp.int32, (tok, tok), 1)
    same_chunk = (trow // c) == (tcol // c)
    lower = jnp.where(jnp.logical_and(same_chunk, trow >= tcol), 1.0, 0.0)
    upper = jnp.where(jnp.logical_and(same_chunk, trow <= tcol), 1.0, 0.0)
    gc_all = _dot_f32(lower, g_all)
    gct_all = _dot_f32(g_all.T, upper)
    row = lax.broadcasted_iota(jnp.int32, (c, c), 0)
    col = lax.broadcasted_iota(jnp.int32, (c, c), 1)
    incl = row >= col
    strict = row > col
    eye = (row == col).astype(F32)
    ogain = ogain_ref[...]

    heads = range(GDN_HEADS)
    chunks = range(GDN_STEP_CHUNKS)
    items = [(ci, h) for ci in chunks for h in heads]

    q_full, k_full, v_full = [], [], []
    for h in heads:
        lo = h * GDN_D
        qf = conv_act(lo)
        kf = conv_act(GDN_WIDTH + lo)
        q_full.append(qf * lax.rsqrt(jnp.sum(qf * qf, axis=-1, keepdims=True) + EPS) * (GDN_D ** -0.5))
        k_full.append(kf * lax.rsqrt(jnp.sum(kf * kf, axis=-1, keepdims=True) + EPS))
        v_full.append(conv_act(2 * GDN_WIDTH + lo))

    pre = {}
    for ci, h in items:
        r0 = ci * c
        gl = GDN_HEADS + h
        qh = q_full[h][r0:r0 + c]
        kh = k_full[h][r0:r0 + c]
        vh = v_full[h][r0:r0 + c]
        beta = beta_all[r0:r0 + c, h:h + 1]
        gcol = gc_all[r0:r0 + c, gl:gl + 1]
        grow = gct_all[gl:gl + 1, r0:r0 + c]
        glast = gc_all[r0 + c - 1:r0 + c, gl:gl + 1]
        decay = jnp.where(incl, jnp.exp(jnp.where(incl, gcol - grow, 0.0)), 0.0)
        kb = kh * beta
        khb = kh.astype(BF16)
        egc = jnp.exp(gcol)
        pre[ci, h] = dict(
            p=-jnp.where(strict, _dot_nt(kb.astype(BF16), khb) * decay, 0.0),
            att=jnp.where(incl, _dot_nt(qh.astype(BF16), khb) * decay, 0.0).astype(BF16),
            rhs=jnp.concatenate([vh * beta, kb * egc], axis=1).astype(BF16),
            qg=qh * egc,
            ktail_t=(kh * jnp.exp(glast - gcol)).T.astype(BF16),
            sdecay=jnp.exp(glast))

    tinv = {it: eye + pre[it]["p"] for it in items}
    pw = {it: pre[it]["p"] for it in items}
    for _ in range(5):
        for it in items:
            pb = pw[it].astype(BF16)
            pw[it] = _dot(pb, pb)
        for it in items:
            tinv[it] = tinv[it] + _dot(tinv[it].astype(BF16), pw[it].astype(BF16))
    sol = {it: _dot(tinv[it].astype(BF16), pre[it]["rhs"]) for it in items}

    drow = lax.broadcasted_iota(jnp.int32, (GDN_D, GDN_D), 0)
    dcol = lax.broadcasted_iota(jnp.int32, (GDN_D, GDN_D), 1)
    eye_d = (drow == dcol).astype(F32)
    lin = {}
    for it in items:
        pr = pre[it]
        solb = sol[it].astype(BF16)
        att_sol = _dot(pr["att"], solb)
        kt_sol = _dot(pr["ktail_t"], solb)
        lin[it] = ((pr["qg"] - att_sol[:, GDN_D:]).astype(BF16), att_sol[:, :GDN_D],
                   (eye_d * pr["sdecay"] - kt_sol[:, GDN_D:]).astype(BF16), kt_sol[:, :GDN_D])
    s_cur = [state[h] for h in heads]
    for ci in chunks:
        r0 = ci * c
        for h in heads:
            q_lin, o_const, s_lin, s_const = lin[ci, h]
            lo = h * GDN_D
            sb = s_cur[h].astype(BF16)
            o = _dot(q_lin, sb) + o_const
            s_cur[h] = _dot(s_lin, sb) + s_const
            gate = gate_ref[r0:r0 + c, lo:lo + GDN_D]
            o = o * _rms_scale(o) * ogain * (gate * _sigmoid(gate))
            o_ref[r0:r0 + c, lo:lo + GDN_D] = o.astype(o_ref.dtype)
    for h in heads:
        state[h] = s_cur[h]
    xbuf[0:CONV_TAIL, :] = xbuf[tok:tok + CONV_TAIL, :]


def _gdn(proj, conv_w, a_log, dt_bias, o_gain, bsz, seq):
    t = proj.shape[0]
    n_s = seq // GDN_TOK
    qkv_w = 3 * GDN_WIDTH
    pad = jnp.zeros((GDN_HEADS,), F32)
    alog_row = jnp.concatenate([pad, a_log.astype(F32), jnp.zeros((LANES - 2 * GDN_HEADS,), F32)]).reshape(1, LANES)
    dtb_row = jnp.concatenate([pad, dt_bias.astype(F32), jnp.zeros((LANES - 2 * GDN_HEADS,), F32)]).reshape(1, LANES)
    return pl.pallas_call(
        _gdn_kernel,
        grid=(bsz, n_s),
        in_specs=[pl.BlockSpec((GDN_TOK, qkv_w), lambda b, s: (b * n_s + s, 0)),
                  pl.BlockSpec((GDN_TOK, GDN_WIDTH), lambda b, s: (b * n_s + s, qkv_w // GDN_WIDTH)),
                  pl.BlockSpec((GDN_TOK, LANES), lambda b, s: (b * n_s + s, (A_IN_PAD - LANES) // LANES)),
                  pl.BlockSpec((CONV_K, qkv_w), lambda b, s: (0, 0)),
                  pl.BlockSpec((1, LANES), lambda b, s: (0, 0)),
                  pl.BlockSpec((1, LANES), lambda b, s: (0, 0)),
                  pl.BlockSpec((1, GDN_D), lambda b, s: (0, 0))],
        out_specs=pl.BlockSpec((GDN_TOK, GDN_WIDTH), lambda b, s: (b * n_s + s, 0)),
        out_shape=jax.ShapeDtypeStruct((t, GDN_WIDTH), BF16),
        scratch_shapes=[pltpu.VMEM((GDN_TOK + CONV_TAIL, qkv_w), F32),
                        pltpu.VMEM((GDN_HEADS, GDN_D, GDN_D), F32)],
        compiler_params=pltpu.CompilerParams(dimension_semantics=("parallel", "arbitrary"),
                                             vmem_limit_bytes=VMEM_LIMIT),
        name="gdn",
    )(proj, proj, proj, conv_w.astype(F32), alog_row, dtb_row, o_gain.astype(F32).reshape(1, GDN_D))


def _mix_out_kernel(h_ref, o_ref, mq_ref, mk_ref, mv_ref, wo_ref, wm_ref, out_ref):
    mq = (mq_ref[...] * (MEM_DH ** -0.5)).astype(BF16)
    mk = mk_ref[...]
    mv = mv_ref[...]
    head = lax.broadcasted_iota(jnp.int32, mk.shape, 1) // MEM_DH
    heads = range(MEM_HEADS)
    k_heads = [jnp.where(head == hh, mk, 0.0).astype(BF16) for hh in heads]
    v_heads = [jnp.where(head == hh, mv, 0.0).astype(BF16) for hh in heads]
    scores = [_dot_nt(mq, k_heads[hh]) for hh in heads]
    probs = []
    for sc in scores:
        p = jnp.exp(sc - jnp.max(sc, axis=-1, keepdims=True))
        probs.append((p * (1.0 / jnp.sum(p, axis=-1, keepdims=True))).astype(BF16))
    m = _dot(probs[0], v_heads[0])
    for hh in heads[1:]:
        m = m + _dot(probs[hh], v_heads[hh])
    y = _dot(o_ref[...].astype(BF16), wo_ref[...]) + _dot(m.astype(BF16), wm_ref[...])
    out_ref[...] = h_ref[...] + y


def _mix_out(h2d, o, proj, mq_block, mkv, w_out, bsz, seq, ts):
    t, d = h2d.shape
    ts = min(ts, seq)
    n_s = seq // ts
    m_len = mkv.shape[0] // bsz
    ow = o.shape[1]
    return pl.pallas_call(
        _mix_out_kernel,
        grid=(bsz, n_s),
        in_specs=[pl.BlockSpec((ts, d), lambda b, s: (b * n_s + s, 0)),
                  pl.BlockSpec((ts, ow), lambda b, s: (b * n_s + s, 0)),
                  pl.BlockSpec((ts, MEM_WIDTH), lambda b, s: (b * n_s + s, mq_block)),
                  pl.BlockSpec((m_len, MEM_WIDTH), lambda b, s: (b, 0)),
                  pl.BlockSpec((m_len, MEM_WIDTH), lambda b, s: (b, 1)),
                  pl.BlockSpec((ow, d), lambda b, s: (0, 0)),
                  pl.BlockSpec((MEM_WIDTH, d), lambda b, s: (0, 0))],
        out_specs=pl.BlockSpec((ts, d), lambda b, s: (b * n_s + s, 0)),
        out_shape=jax.ShapeDtypeStruct((t, d), F32),
        compiler_params=pltpu.CompilerParams(dimension_semantics=("parallel", "parallel"),
                                             vmem_limit_bytes=VMEM_LIMIT),
        name="mix_out",
    )(h2d, o, proj, mkv, mkv, w_out[:ow].astype(BF16), w_out[ow:].astype(BF16))


SB_SUB = 8
SB_NEG = -1e30
SB_DONE = 88.0
SB_HEAD_ROWS = 16


def _split_bf16(x):
    hi = pltpu.bitcast(pltpu.bitcast(x, jnp.uint32) & jnp.uint32(0xFFFF0000), F32)
    return hi.astype(BF16), (x - hi).astype(BF16)


def _sb_kernel(q_ref, k_ref, v_ref, o_ref):
    blk = SB_BLOCK
    step = pl.program_id(2)
    lane = lax.broadcasted_iota(jnp.int32, (blk, 2 * SB_DH), 1)
    row1 = lax.broadcasted_iota(jnp.int32, (blk, blk), 0)
    col1 = lax.broadcasted_iota(jnp.int32, (blk, blk), 1)
    causal = col1 < row1
    srow2 = lax.broadcasted_iota(jnp.int32, (2 * blk, 2 * blk), 0)
    scol2 = lax.broadcasted_iota(jnp.int32, (2 * blk, 2 * blk), 1)
    suffix2 = (srow2 >= scol2).astype(BF16)
    suffix1 = suffix2[:blk, :blk]
    suffix2x2 = jnp.concatenate([suffix2, suffix2], axis=0)
    suffix1x2 = jnp.concatenate([suffix1, suffix1], axis=0)

    def suffix_sums(sp, suffix_x2):
        return _dot(jnp.concatenate(_split_bf16(sp), axis=1), suffix_x2)

    q_heads, k2s, v2s = [], [], []
    for j in range(SB_SUB):
        i = step * SB_SUB + j
        q = q_ref[j * blk:(j + 1) * blk, :] * (SB_DH ** -0.5)
        q_heads.append(jnp.where(lane < SB_DH, q, 0.0).astype(BF16))
        q_heads.append(jnp.where(lane >= SB_DH, q, 0.0).astype(BF16))
        prev = pl.multiple_of(jnp.maximum(i - 1, 0) * blk, blk)
        diag = pl.multiple_of(i * blk, blk)
        k2s.append(jnp.concatenate([k_ref[pl.ds(prev, blk), :], k_ref[pl.ds(diag, blk), :]], axis=0))
        has_prev = jnp.where(i > 0, 1.0, 0.0).astype(BF16)
        v2s.append(jnp.concatenate([v_ref[pl.ds(prev, blk), :] * has_prev, v_ref[pl.ds(diag, blk), :]], axis=0))
    def older_block(offset):
        kblks, vblks = [], []
        for j in range(SB_SUB):
            kb = step * SB_SUB + j - offset
            start = pl.multiple_of(jnp.maximum(kb, 0) * blk, blk)
            kblks.append(k_ref[pl.ds(start, blk), :])
            vblks.append(v_ref[pl.ds(start, blk), :] * jnp.where(kb >= 0, 1.0, 0.0).astype(BF16))
        return kblks, vblks

    hr = SB_HEAD_ROWS
    k3s, v3s = older_block(2)
    chains = range(2 * SB_SUB)
    raw = [_dot_nt(q_heads[n], k2s[n // 2]) for n in chains]
    z3 = [_dot_nt(q_heads[n][:hr], k3s[n // 2]) for n in chains]
    zs = [jnp.concatenate([raw[n][:, :blk], jnp.where(causal, raw[n][:, blk:], SB_NEG)], axis=1) for n in chains]
    sums = [suffix_sums(_softplus(zs[n]), suffix2x2) for n in chains]
    s3 = [suffix_sums(_softplus(z3[n]), suffix1x2) for n in chains]
    probs = [jnp.exp(zs[n] - sums[n]).astype(BF16) for n in chains]
    carries = [sums[n][:, 0:1] for n in chains]
    p3 = [jnp.exp(z3[n] - carries[n][:hr] - s3[n]).astype(BF16) for n in chains]
    accs = [_dot(probs[n], v2s[n // 2]) for n in chains]
    accs = [jnp.concatenate([accs[n][:hr] + _dot(p3[n], v3s[n // 2]), accs[n][hr:]], axis=0) for n in chains]
    carries = [jnp.concatenate([carries[n][:hr] + s3[n][:, 0:1], carries[n][hr:]], axis=0) for n in chains]
    head_done = row1 < hr

    def lowest(cs):
        m = cs[0]
        for cc in cs[1:]:
            m = jnp.minimum(m, cc)
        return jnp.min(m)

    def more(st):
        it, cmin = st[0], st[1]
        return jnp.logical_and(it < (step + 1) * SB_SUB - 2, cmin < SB_DONE)

    def body(st):
        it = st[0]
        cs = list(st[2:2 + 2 * SB_SUB])
        acs = list(st[2 + 2 * SB_SUB:])
        kblks, vblks = older_block(2 + it)
        skip = jnp.logical_and(it == 0, head_done)
        zl = [jnp.where(skip, SB_NEG, _dot_nt(q_heads[n], kblks[n // 2])) for n in chains]
        sl = [suffix_sums(_softplus(zl[n]), suffix1x2) for n in chains]
        pl_ = [jnp.exp(zl[n] - cs[n] - sl[n]).astype(BF16) for n in chains]
        acs = [acs[n] + _dot(pl_[n], vblks[n // 2]) for n in chains]
        cs = [cs[n] + sl[n][:, 0:1] for n in chains]
        return (it + 1, lowest(cs), *cs, *acs)

    st = lax.while_loop(more, body, (jnp.int32(0), lowest(carries), *carries, *accs))
    accs = st[2 + 2 * SB_SUB:]
    for j in range(SB_SUB):
        o_ref[j * blk:(j + 1) * blk, :] = jnp.where(lane < SB_DH, accs[2 * j], accs[2 * j + 1]).astype(o_ref.dtype)


def _stick_breaking(qproj, kv, bsz, seq):
    t = qproj.shape[0]
    tq = SB_SUB * SB_BLOCK
    n_q = seq // tq
    pairs = SB_HEADS // 2
    pw = 2 * SB_DH
    return pl.pallas_call(
        _sb_kernel,
        grid=(bsz, pairs, n_q),
        in_specs=[pl.BlockSpec((tq, pw), lambda b, p, i: (b * n_q + i, p)),
                  pl.BlockSpec((seq, pw), lambda b, p, i: (b, p)),
                  pl.BlockSpec((seq, pw), lambda b, p, i: (b, pairs + p))],
        out_specs=pl.BlockSpec((tq, pw), lambda b, p, i: (b * n_q + i, p)),
        out_shape=jax.ShapeDtypeStruct((t, SB_WIDTH), BF16),
        compiler_params=pltpu.CompilerParams(dimension_semantics=("parallel", "parallel", "arbitrary"),
                                             vmem_limit_bytes=VMEM_LIMIT),
        name="stick_breaking",
    )(qproj, kv, kv)


MOE_TM = 1024
MOE_SUB = 320


def _split3(x):
    hi = x.astype(BF16)
    r1 = x - hi.astype(F32)
    mid = r1.astype(BF16)
    return hi, mid, (r1 - mid.astype(F32)).astype(BF16)


def _moe_kernel(h_ref, g_ref, wrh_ref, wrl_ref, br_ref, sel_ref, w13_ref, w2_ref, fg_ref, out_ref,
                xn_sc, cw_sc, mcol_sc, mrow_sc, cnt_sc, acc_sc, tril_sc, triu_sc, *, final_norm):
    tile = pl.program_id(0)
    grp = pl.program_id(1)
    tm = h_ref.shape[0]

    @pl.when(jnp.logical_and(tile == 0, grp == 0))
    def _():
        r = lax.broadcasted_iota(jnp.int32, (tm, tm), 0)
        c = lax.broadcasted_iota(jnp.int32, (tm, tm), 1)
        tril_sc[...] = jnp.where(r > c, 1.0, 0.0).astype(BF16)
        triu_sc[...] = jnp.where(r < c, 1.0, 0.0).astype(BF16)

    @pl.when(grp == 0)
    def _():
        x = h_ref[...]
        xn = x * _rms_scale(x) * g_ref[...]
        xh = xn.astype(BF16)
        xn_sc[...] = xh
        xl = (xn - xh.astype(F32)).astype(BF16)
        logits = _dot(xh, wrh_ref[...]) + _dot(xl, wrh_ref[...]) + _dot(xh, wrl_ref[...]) + br_ref[...]
        lane = lax.broadcasted_iota(jnp.int32, (tm, LANES), 1)
        lane_f = lane.astype(F32)
        big = float(LANES)
        is_g = lane < N_GROUPS
        gl = jnp.where(is_g, logits, -jnp.inf)
        gmax = jnp.max(gl, axis=-1, keepdims=True)
        p_group = 1.0 / jnp.sum(jnp.where(is_g, jnp.exp(logits - gmax), 0.0), axis=-1, keepdims=True)
        gsel = jnp.min(jnp.where(gl == gmax, lane_f, big), axis=-1, keepdims=True)
        e_lane = lane - ROUTER_LANE0
        grp_of_lane = (e_lane // EXPERTS_PER_GROUP).astype(F32)
        in_grp = jnp.logical_and(jnp.logical_and(e_lane >= 0, e_lane < N_EXPERTS), grp_of_lane == gsel)
        el = jnp.where(in_grp, logits, -jnp.inf)
        emax = jnp.max(el, axis=-1, keepdims=True)
        ee = jnp.where(in_grp, jnp.exp(logits - emax), 0.0)
        prob = ee / jnp.sum(ee, axis=-1, keepdims=True)
        pm = jnp.where(in_grp, prob, -1.0)
        p1 = jnp.max(pm, axis=-1, keepdims=True)
        i1 = jnp.min(jnp.where(pm == p1, lane_f, big), axis=-1, keepdims=True)
        pm2 = jnp.where(lane_f == i1, -1.0, pm)
        p2 = jnp.max(pm2, axis=-1, keepdims=True)
        i2 = jnp.min(jnp.where(pm2 == p2, lane_f, big), axis=-1, keepdims=True)
        sel = jnp.logical_or(lane_f == i1, lane_f == i2)
        comb = jnp.where(sel, p_group * (prob / (p1 + p2)), 0.0)
        ch, cm, cl = _split3(comb)
        lane_sel = sel_ref[...]
        comb4 = _dot(ch, lane_sel) + _dot(cm, lane_sel) + _dot(cl, lane_sel)
        c4h, c4m, c4l = _split3(comb4)
        cw_sc[:, 0:LANES] = c4h
        cw_sc[:, LANES:2 * LANES] = c4m
        cw_sc[:, 2 * LANES:3 * LANES] = c4l
        onehot = jnp.where(lane_f == gsel, 1.0, 0.0)
        rank_col = jnp.sum(onehot * _dot(tril_sc[...], onehot.astype(BF16)), axis=-1, keepdims=True)
        mcol_sc[...] = jnp.where(lane == 0, gsel, jnp.where(lane == 1, rank_col, 0.0))
        cnt_sc[...] = jnp.broadcast_to(jnp.sum(onehot, axis=0, keepdims=True), cnt_sc.shape)
        oht = onehot.T
        cumt = _dot(oht.astype(BF16), triu_sc[...])
        sub = lax.broadcasted_iota(jnp.int32, (LANES, tm), 0)
        rank_row = jnp.sum(oht * cumt, axis=0, keepdims=True)
        grp_row = jnp.sum(oht * sub.astype(F32), axis=0, keepdims=True)
        srow = lax.broadcasted_iota(jnp.int32, mrow_sc.shape, 0)
        mrow_sc[...] = jnp.where(srow == 0, grp_row, jnp.where(srow == 1, rank_row, 0.0))
        acc_sc[...] = jnp.zeros(acc_sc.shape, F32)

    grp_f = grp.astype(F32)
    cnt_lane = lax.broadcasted_iota(jnp.int32, (1, LANES), 1)
    n_rows = jnp.sum(jnp.where(cnt_lane == grp, cnt_sc[0:1, :], 0.0)).astype(jnp.int32)
    rel_row = jnp.where(mrow_sc[0:1, :] == grp_f, mrow_sc[1:2, :], -1.0)
    mcol = mcol_sc[...]
    rel_col = jnp.where(mcol[:, 0:1] == grp_f, mcol[:, 1:2], -1.0)
    r_iota = lax.broadcasted_iota(jnp.int32, (MOE_SUB, tm), 0).astype(F32)
    c_iota = lax.broadcasted_iota(jnp.int32, (tm, MOE_SUB), 1).astype(F32)

    def sub_tile(s, carry):
        base = (s * MOE_SUB).astype(F32)
        gather = jnp.where(r_iota + base == rel_row, 1.0, 0.0).astype(BF16)
        xs = _dot(gather, xn_sc[...]).astype(BF16)
        cw3 = _dot(gather, cw_sc[...])
        cw = cw3[:, 0:LANES] + cw3[:, LANES:2 * LANES] + cw3[:, 2 * LANES:3 * LANES]
        y = jnp.zeros((MOE_SUB, out_ref.shape[1]), F32)
        for e in range(EXPERTS_PER_GROUP):
            hcat = _dot(xs, w13_ref[e])
            hg = hcat[:, :EXPERT_FF]
            hid = hg * _sigmoid(hg) * hcat[:, EXPERT_FF:]
            y = y + _dot((hid * cw[:, e:e + 1]).astype(BF16), w2_ref[e])
        scatter = jnp.where(c_iota + base == rel_col, 1.0, 0.0).astype(BF16)
        acc_sc[...] += _dot(scatter, y.astype(BF16))
        return carry

    lax.fori_loop(0, (n_rows + MOE_SUB - 1) // MOE_SUB, sub_tile, 0)

    @pl.when(grp == N_GROUPS - 1)
    def _():
        y = h_ref[...] + acc_sc[...]
        if final_norm:
            y = y * _rms_scale(y) * fg_ref[...]
        out_ref[...] = y


def _moe(h2d, ffn_g, w_group, b_group, w_router, b_router, w1, w3, w2, final_g, final_norm):
    t, d = h2d.shape
    tm = min(MOE_TM, t)
    pad = LANES - N_GROUPS - N_EXPERTS
    wr = jnp.concatenate([w_group, w_router, jnp.zeros((d, pad), F32)], axis=1).astype(F32)
    wr_hi = wr.astype(BF16)
    wr_lo = (wr - wr_hi.astype(F32)).astype(BF16)
    br = jnp.concatenate([b_group, b_router, jnp.zeros((pad,), F32)]).astype(F32).reshape(1, LANES)
    src = jnp.arange(LANES)[:, None]
    dst = jnp.arange(LANES)[None, :]
    is_expert = jnp.logical_and(src >= ROUTER_LANE0, src < ROUTER_LANE0 + N_EXPERTS)
    lane_sel = jnp.logical_and(is_expert, (src - ROUTER_LANE0) % EXPERTS_PER_GROUP == dst).astype(BF16)
    w13 = jnp.concatenate([w1, w3], axis=-1).astype(BF16)
    const = lambda i, g: (0, 0)
    return pl.pallas_call(
        functools.partial(_moe_kernel, final_norm=final_norm),
        grid=(t // tm, N_GROUPS),
        in_specs=[pl.BlockSpec((tm, d), lambda i, g: (i, 0)),
                  pl.BlockSpec((1, d), const),
                  pl.BlockSpec((d, LANES), const),
                  pl.BlockSpec((d, LANES), const),
                  pl.BlockSpec((1, LANES), const),
                  pl.BlockSpec((LANES, LANES), const),
                  pl.BlockSpec((EXPERTS_PER_GROUP, d, 2 * EXPERT_FF), lambda i, g: (g, 0, 0)),
                  pl.BlockSpec((EXPERTS_PER_GROUP, EXPERT_FF, d), lambda i, g: (g, 0, 0)),
                  pl.BlockSpec((1, d), const)],
        out_specs=pl.BlockSpec((tm, d), lambda i, g: (i, 0)),
        out_shape=jax.ShapeDtypeStruct((t, d), F32),
        scratch_shapes=[pltpu.VMEM((tm, d), BF16),
                        pltpu.VMEM((tm, 3 * LANES), BF16),
                        pltpu.VMEM((tm, LANES), F32),
                        pltpu.VMEM((8, tm), F32),
                        pltpu.VMEM((8, LANES), F32),
                        pltpu.VMEM((tm, d), F32),
                        pltpu.VMEM((tm, tm), BF16), pltpu.VMEM((tm, tm), BF16)],
        compiler_params=pltpu.CompilerParams(dimension_semantics=("arbitrary", "arbitrary"),
                                             vmem_limit_bytes=VMEM_LIMIT),
        name="moe",
    )(h2d, ffn_g.reshape(1, d), wr_hi, wr_lo, br, lane_sel, w13, w2.astype(BF16), final_g.reshape(1, d))


def kernel(x, mem, a_norm, a_w_in, a_conv, a_log, a_dt_bias, a_out_gain, a_w_out, kv_norm, w_kv, b_norm, b_w_in,
           b_w_out, mem_norm, w_mem_kv, ffn_norm, w_group, b_group, w_router, b_router, w1, w3, w2, final_norm):
    bsz, seq, d = x.shape
    assert seq % (SB_SUB * SB_BLOCK) == 0 and seq % GDN_TOK == 0 and seq % MIX_TS == 0, seq
    m_len = mem.shape[1]
    depth = mem_norm.shape[0]
    n_a = a_norm.shape[0]
    h = x.reshape(bsz * seq, d)
    mem2d = mem.reshape(bsz * m_len, d)
    kv = None
    for l in range(depth):
        mkv = _norm_matmul(mem2d, mem_norm[l], w_mem_kv[l], F32, PROJ_TM)
        if l < n_a:
            w_in = a_w_in[l]
            g4 = 4 * GDN_WIDTH
            w_perm = jnp.concatenate([w_in[:, :g4], w_in[:, g4 + 2 * GDN_HEADS:], w_in[:, g4:g4 + 2 * GDN_HEADS],
                                      jnp.zeros((d, LANES - 2 * GDN_HEADS), w_in.dtype)], axis=1)
            proj = _norm_matmul(h, a_norm[l], w_perm, F32, PROJ_TM)
            o = _gdn(proj, a_conv[l], a_log[l], a_dt_bias[l], a_out_gain[l], bsz, seq)
            h = _mix_out(h, o, proj, g4 // MEM_WIDTH, mkv, a_w_out[l], bsz, seq, MIX_TS)
        else:
            lb = l - n_a
            if l == n_a:
                kv, proj = _norm_matmul2(h, kv_norm, w_kv, b_norm[lb], b_w_in[lb], BF16, PROJ_TM)
            else:
                proj = _norm_matmul(h, b_norm[lb], b_w_in[lb], BF16, PROJ_TM)
            o = _stick_breaking(proj, kv, bsz, seq)
            h = _mix_out(h, o, proj, SB_WIDTH // MEM_WIDTH, mkv, b_w_out[lb], bsz, seq, MIX_TS)
        h = _moe(h, ffn_norm[l], w_group[l], b_group[l], w_router[l], b_router[l], w1[l], w3[l], w2[l],
                 final_norm, l == depth - 1)
    return h.reshape(bsz, seq, d)
```

```python
import functools

import jax
import jax.numpy as jnp
from jax import lax
from jax.experimental import pallas as pl
from jax.experimental.pallas import tpu as pltpu

EPS = 1e-6
GDN_HEADS = 6
GDN_D = 128
GDN_WIDTH = GDN_HEADS * GDN_D
CONV_K = 4
CHUNK = 64
SB_HEADS = 12
SB_DH = 64
SB_WIDTH = SB_HEADS * SB_DH
SB_BLOCK = 128
MEM_HEADS = 4
MEM_DH = 64
MEM_WIDTH = MEM_HEADS * MEM_DH
N_GROUPS = 4
EXPERTS_PER_GROUP = 4
N_EXPERTS = N_GROUPS * EXPERTS_PER_GROUP
EXPERT_FF = 256
LANES = 128
A_IN_PAD = 4 * GDN_WIDTH + MEM_WIDTH + LANES
ROUTER_LANE0 = N_GROUPS
VMEM_LIMIT = 48 * 1024 * 1024
PROJ_TM = 512
MIX_TS = 256

F32 = jnp.float32
BF16 = jnp.bfloat16


def _dot(a, b):
    return jnp.dot(a, b, preferred_element_type=F32)


def _dot_nt(a, b):
    return lax.dot_general(a, b, (((1,), (1,)), ((), ())), preferred_element_type=F32)


def _dot_f32(a, b):
    return jnp.dot(a, b, preferred_element_type=F32, precision=lax.Precision.HIGHEST)


def _sigmoid(x):
    return 1.0 / (1.0 + jnp.exp(-x))


def _softplus(x):
    return jnp.maximum(x, 0.0) + jnp.log(1.0 + jnp.exp(-jnp.abs(x)))


def _rms_scale(x):
    return lax.rsqrt(jnp.mean(x * x, axis=-1, keepdims=True) + EPS)


def _norm_matmul_kernel(x_ref, g_ref, w_ref, o_ref, *, col_chunk):
    x = x_ref[...]
    xn = (x * _rms_scale(x) * g_ref[...]).astype(BF16)
    n = o_ref.shape[-1]
    for c0 in range(0, n, col_chunk):
        c1 = min(n, c0 + col_chunk)
        o_ref[:, c0:c1] = _dot(xn, w_ref[:, c0:c1]).astype(o_ref.dtype)


def _norm_matmul(x2d, g, w, out_dtype, tm):
    t, d = x2d.shape
    n = w.shape[1]
    tm = min(tm, t)
    return pl.pallas_call(
        functools.partial(_norm_matmul_kernel, col_chunk=4 * LANES),
        grid=(t // tm,),
        in_specs=[pl.BlockSpec((tm, d), lambda i: (i, 0)),
                  pl.BlockSpec((1, d), lambda i: (0, 0)),
                  pl.BlockSpec((d, n), lambda i: (0, 0))],
        out_specs=pl.BlockSpec((tm, n), lambda i: (i, 0)),
        out_shape=jax.ShapeDtypeStruct((t, n), out_dtype),
        compiler_params=pltpu.CompilerParams(dimension_semantics=("parallel",), vmem_limit_bytes=VMEM_LIMIT),
        name="norm_matmul",
    )(x2d, g.reshape(1, d), w.astype(BF16))


def _norm_matmul2_kernel(x_ref, ga_ref, wa_ref, gb_ref, wb_ref, oa_ref, ob_ref, *, col_chunk):
    x = x_ref[...]
    xs = x * _rms_scale(x)
    for g_ref, w_ref, o_ref in ((ga_ref, wa_ref, oa_ref), (gb_ref, wb_ref, ob_ref)):
        xn = (xs * g_ref[...]).astype(BF16)
        n = o_ref.shape[-1]
        for c0 in range(0, n, col_chunk):
            c1 = min(n, c0 + col_chunk)
            o_ref[:, c0:c1] = _dot(xn, w_ref[:, c0:c1]).astype(o_ref.dtype)


def _norm_matmul2(x2d, ga, wa, gb, wb, out_dtype, tm):
    t, d = x2d.shape
    na, nb = wa.shape[1], wb.shape[1]
    tm = min(tm, t)
    const = lambda i: (0, 0)
    return pl.pallas_call(
        functools.partial(_norm_matmul2_kernel, col_chunk=4 * LANES),
        grid=(t // tm,),
        in_specs=[pl.BlockSpec((tm, d), lambda i: (i, 0)),
                  pl.BlockSpec((1, d), const), pl.BlockSpec((d, na), const),
                  pl.BlockSpec((1, d), const), pl.BlockSpec((d, nb), const)],
        out_specs=[pl.BlockSpec((tm, na), lambda i: (i, 0)), pl.BlockSpec((tm, nb), lambda i: (i, 0))],
        out_shape=[jax.ShapeDtypeStruct((t, na), out_dtype), jax.ShapeDtypeStruct((t, nb), out_dtype)],
        compiler_params=pltpu.CompilerParams(dimension_semantics=("parallel",), vmem_limit_bytes=VMEM_LIMIT),
        name="norm_matmul2",
    )(x2d, ga.reshape(1, d), wa.astype(BF16), gb.reshape(1, d), wb.astype(BF16))


GDN_STEP_CHUNKS = 4
GDN_TOK = CHUNK * GDN_STEP_CHUNKS
CONV_TAIL = 8


def _gdn_kernel(qkv_ref, gate_ref, ba_ref, convw_ref, alog_ref, dtb_ref, ogain_ref, o_ref, xbuf, state):
    c = CHUNK
    tok = GDN_TOK

    @pl.when(pl.program_id(1) == 0)
    def _():
        xbuf[0:CONV_TAIL, :] = jnp.zeros((CONV_TAIL, xbuf.shape[1]), F32)
        state[...] = jnp.zeros(state.shape, F32)

    xbuf[CONV_TAIL:CONV_TAIL + tok, :] = qkv_ref[...]

    def conv_act(lo):
        win = xbuf[:, lo:lo + GDN_D]
        acc = win[CONV_TAIL:] * convw_ref[CONV_K - 1:CONV_K, lo:lo + GDN_D]
        for j in range(CONV_K - 1):
            shifted = pltpu.roll(win, CONV_K - 1 - j, axis=0)[CONV_TAIL:]
            acc = acc + shifted * convw_ref[j:j + 1, lo:lo + GDN_D]
        return acc * _sigmoid(acc)

    ba = ba_ref[...]
    beta_all = _sigmoid(ba)
    g_all = -jnp.exp(alog_ref[...]) * _softplus(ba + dtb_ref[...])
    trow = lax.broadcasted_iota(jnp.int32, (tok, tok), 0)
    tcol = lax.broadcasted_iota(jnp.int32, (tok, tok), 1)
    same_chunk = (trow // c) == (tcol // c)
    lower = jnp.where(jnp.logical_and(same_chunk, trow >= tcol), 1.0, 0.0)
    upper = jnp.where(jnp.logical_and(same_chunk, trow <= tcol), 1.0, 0.0)
    gc_all = _dot_f32(lower, g_all)
    gct_all = _dot_f32(g_all.T, upper)
    row = lax.broadcasted_iota(jnp.int32, (c, c), 0)
    col = lax.broadcasted_iota(jnp.int32, (c, c), 1)
    incl = row >= col
    strict = row > col
    eye = (row == col).astype(F32)
    ogain = ogain_ref[...]

    heads = range(GDN_HEADS)
    chunks = range(GDN_STEP_CHUNKS)
    items = [(ci, h) for ci in chunks for h in heads]

    q_full, k_full, v_full = [], [], []
    for h in heads:
        lo = h * GDN_D
        qf = conv_act(lo)
        kf = conv_act(GDN_WIDTH + lo)
        q_full.append(qf * lax.rsqrt(jnp.sum(qf * qf, axis=-1, keepdims=True) + EPS) * (GDN_D ** -0.5))
        k_full.append(kf * lax.rsqrt(jnp.sum(kf * kf, axis=-1, keepdims=True) + EPS))
        v_full.append(conv_act(2 * GDN_WIDTH + lo))

    pre = {}
    for ci, h in items:
        r0 = ci * c
        gl = GDN_HEADS + h
        qh = q_full[h][r0:r0 + c]
        kh = k_full[h][r0:r0 + c]
        vh = v_full[h][r0:r0 + c]
        beta = beta_all[r0:r0 + c, h:h + 1]
        gcol = gc_all[r0:r0 + c, gl:gl + 1]
        grow = gct_all[gl:gl + 1, r0:r0 + c]
        glast = gc_all[r0 + c - 1:r0 + c, gl:gl + 1]
        decay = jnp.where(incl, jnp.exp(jnp.where(incl, gcol - grow, 0.0)), 0.0)
        kb = kh * beta
        khb = kh.astype(BF16)
        egc = jnp.exp(gcol)
        pre[ci, h] = dict(
            p=-jnp.where(strict, _dot_nt(kb.astype(BF16), khb) * decay, 0.0),
            att=jnp.where(incl, _dot_nt(qh.astype(BF16), khb) * decay, 0.0).astype(BF16),
            rhs=jnp.concatenate([vh * beta, kb * egc], axis=1).astype(BF16),
            qg=qh * egc,
            ktail_t=(kh * jnp.exp(glast - gcol)).T.astype(BF16),
            sdecay=jnp.exp(glast))

    tinv = {it: eye + pre[it]["p"] for it in items}
    pw = {it: pre[it]["p"] for it in items}
    for _ in range(5):
        for it in items:
            pb = pw[it].astype(BF16)
            pw[it] = _dot(pb, pb)
        for it in items:
            tinv[it] = tinv[it] + _dot(tinv[it].astype(BF16), pw[it].astype(BF16))
    sol = {it: _dot(tinv[it].astype(BF16), pre[it]["rhs"]) for it in items}

    drow = lax.broadcasted_iota(jnp.int32, (GDN_D, GDN_D), 0)
    dcol = lax.broadcasted_iota(jnp.int32, (GDN_D, GDN_D), 1)
    eye_d = (drow == dcol).astype(F32)
    lin = {}
    for it in items:
        pr = pre[it]
        solb = sol[it].astype(BF16)
        att_sol = _dot(pr["att"], solb)
        kt_sol = _dot(pr["ktail_t"], solb)
        lin[it] = ((pr["qg"] - att_sol[:, GDN_D:]).astype(BF16), att_sol[:, :GDN_D],
                   (eye_d * pr["sdecay"] - kt_sol[:, GDN_D:]).astype(BF16), kt_sol[:, :GDN_D])
    s_cur = [state[h] for h in heads]
    for ci in chunks:
        r0 = ci * c
        for h in heads:
            q_lin, o_const, s_lin, s_const = lin[ci, h]
            lo = h * GDN_D
            sb = s_cur[h].astype(BF16)
            o = _dot(q_lin, sb) + o_const
            s_cur[h] = _dot(s_lin, sb) + s_const
            gate = gate_ref[r0:r0 + c, lo:lo + GDN_D]
            o = o * _rms_scale(o) * ogain * (gate * _sigmoid(gate))
            o_ref[r0:r0 + c, lo:lo + GDN_D] = o.astype(o_ref.dtype)
    for h in heads:
        state[h] = s_cur[h]
    xbuf[0:CONV_TAIL, :] = xbuf[tok:tok + CONV_TAIL, :]


def _gdn(proj, conv_w, a_log, dt_bias, o_gain, bsz, seq):
    t = proj.shape[0]
    n_s = seq // GDN_TOK
    qkv_w = 3 * GDN_WIDTH
    pad = jnp.zeros((GDN_HEADS,), F32)
    alog_row = jnp.concatenate([pad, a_log.astype(F32), jnp.zeros((LANES - 2 * GDN_HEADS,), F32)]).reshape(1, LANES)
    dtb_row = jnp.concatenate([pad, dt_bias.astype(F32), jnp.zeros((LANES - 2 * GDN_HEADS,), F32)]).reshape(1, LANES)
    return pl.pallas_call(
        _gdn_kernel,
        grid=(bsz, n_s),
        in_specs=[pl.BlockSpec((GDN_TOK, qkv_w), lambda b, s: (b * n_s + s, 0)),
                  pl.BlockSpec((GDN_TOK, GDN_WIDTH), lambda b, s: (b * n_s + s, qkv_w // GDN_WIDTH)),
                  pl.BlockSpec((GDN_TOK, LANES), lambda b, s: (b * n_s + s, (A_IN_PAD - LANES) // LANES)),
                  pl.BlockSpec((CONV_K, qkv_w), lambda b, s: (0, 0)),
                  pl.BlockSpec((1, LANES), lambda b, s: (0, 0)),
                  pl.BlockSpec((1, LANES), lambda b, s: (0, 0)),
                  pl.BlockSpec((1, GDN_D), lambda b, s: (0, 0))],
        out_specs=pl.BlockSpec((GDN_TOK, GDN_WIDTH), lambda b, s: (b * n_s + s, 0)),
        out_shape=jax.ShapeDtypeStruct((t, GDN_WIDTH), BF16),
        scratch_shapes=[pltpu.VMEM((GDN_TOK + CONV_TAIL, qkv_w), F32),
                        pltpu.VMEM((GDN_HEADS, GDN_D, GDN_D), F32)],
        compiler_params=pltpu.CompilerParams(dimension_semantics=("parallel", "arbitrary"),
                                             vmem_limit_bytes=VMEM_LIMIT),
        name="gdn",
    )(proj, proj, proj, conv_w.astype(F32), alog_row, dtb_row, o_gain.astype(F32).reshape(1, GDN_D))


def _mix_out_kernel(h_ref, o_ref, mq_ref, mk_ref, mv_ref, wo_ref, wm_ref, out_ref):
    mq = (mq_ref[...] * (MEM_DH ** -0.5)).astype(BF16)
    mk = mk_ref[...]
    mv = mv_ref[...]
    head = lax.broadcasted_iota(jnp.int32, mk.shape, 1) // MEM_DH
    heads = range(MEM_HEADS)
    k_heads = [jnp.where(head == hh, mk, 0.0).astype(BF16) for hh in heads]
    v_heads = [jnp.where(head == hh, mv, 0.0).astype(BF16) for hh in heads]
    scores = [_dot_nt(mq, k_heads[hh]) for hh in heads]
    probs = []
    for sc in scores:
        p = jnp.exp(sc - jnp.max(sc, axis=-1, keepdims=True))
        probs.append((p * (1.0 / jnp.sum(p, axis=-1, keepdims=True))).astype(BF16))
    m = _dot(probs[0], v_heads[0])
    for hh in heads[1:]:
        m = m + _dot(probs[hh], v_heads[hh])
    y = _dot(o_ref[...].astype(BF16), wo_ref[...]) + _dot(m.astype(BF16), wm_ref[...])
    out_ref[...] = h_ref[...] + y


def _mix_out(h2d, o, proj, mq_block, mkv, w_out, bsz, seq, ts):
    t, d = h2d.shape
    ts = min(ts, seq)
    n_s = seq // ts
    m_len = mkv.shape[0] // bsz
    ow = o.shape[1]
    return pl.pallas_call(
        _mix_out_kernel,
        grid=(bsz, n_s),
        in_specs=[pl.BlockSpec((ts, d), lambda b, s: (b * n_s + s, 0)),
                  pl.BlockSpec((ts, ow), lambda b, s: (b * n_s + s, 0)),
                  pl.BlockSpec((ts, MEM_WIDTH), lambda b, s: (b * n_s + s, mq_block)),
                  pl.BlockSpec((m_len, MEM_WIDTH), lambda b, s: (b, 0)),
                  pl.BlockSpec((m_len, MEM_WIDTH), lambda b, s: (b, 1)),
                  pl.BlockSpec((ow, d), lambda b, s: (0, 0)),
                  pl.BlockSpec((MEM_WIDTH, d), lambda b, s: (0, 0))],
        out_specs=pl.BlockSpec((ts, d), lambda b, s: (b * n_s + s, 0)),
        out_shape=jax.ShapeDtypeStruct((t, d), F32),
        compiler_params=pltpu.CompilerParams(dimension_semantics=("parallel", "parallel"),
                                             vmem_limit_bytes=VMEM_LIMIT),
        name="mix_out",
    )(h2d, o, proj, mkv, mkv, w_out[:ow].astype(BF16), w_out[ow:].astype(BF16))


SB_SUB = 8
SB_NEG = -1e30
SB_DONE = 88.0
SB_HEAD_ROWS = 16


def _split_bf16(x):
    hi = pltpu.bitcast(pltpu.bitcast(x, jnp.uint32) & jnp.uint32(0xFFFF0000), F32)
    return hi.astype(BF16), (x - hi).astype(BF16)


def _sb_kernel(q_ref, k_ref, v_ref, o_ref):
    blk = SB_BLOCK
    step = pl.program_id(2)
    lane = lax.broadcasted_iota(jnp.int32, (blk, 2 * SB_DH), 1)
    row1 = lax.broadcasted_iota(jnp.int32, (blk, blk), 0)
    col1 = lax.broadcasted_iota(jnp.int32, (blk, blk), 1)
    causal = col1 < row1
    srow2 = lax.broadcasted_iota(jnp.int32, (2 * blk, 2 * blk), 0)
    scol2 = lax.broadcasted_iota(jnp.int32, (2 * blk, 2 * blk), 1)
    suffix2 = (srow2 >= scol2).astype(BF16)
    suffix1 = suffix2[:blk, :blk]
    suffix2x2 = jnp.concatenate([suffix2, suffix2], axis=0)
    suffix1x2 = jnp.concatenate([suffix1, suffix1], axis=0)

    def suffix_sums(sp, suffix_x2):
        return _dot(jnp.concatenate(_split_bf16(sp), axis=1), suffix_x2)

    q_heads, k2s, v2s = [], [], []
    for j in range(SB_SUB):
        i = step * SB_SUB + j
        q = q_ref[j * blk:(j + 1) * blk, :] * (SB_DH ** -0.5)
        q_heads.append(jnp.where(lane < SB_DH, q, 0.0).astype(BF16))
        q_heads.append(jnp.where(lane >= SB_DH, q, 0.0).astype(BF16))
        prev = pl.multiple_of(jnp.maximum(i - 1, 0) * blk, blk)
        diag = pl.multiple_of(i * blk, blk)
        k2s.append(jnp.concatenate([k_ref[pl.ds(prev, blk), :], k_ref[pl.ds(diag, blk), :]], axis=0))
        has_prev = jnp.where(i > 0, 1.0, 0.0).astype(BF16)
        v2s.append(jnp.concatenate([v_ref[pl.ds(prev, blk), :] * has_prev, v_ref[pl.ds(diag, blk), :]], axis=0))
    def older_block(offset):
        kblks, vblks = [], []
        for j in range(SB_SUB):
            kb = step * SB_SUB + j - offset
            start = pl.multiple_of(jnp.maximum(kb, 0) * blk, blk)
            kblks.append(k_ref[pl.ds(start, blk), :])
            vblks.append(v_ref[pl.ds(start, blk), :] * jnp.where(kb >= 0, 1.0, 0.0).astype(BF16))
        return kblks, vblks

    hr = SB_HEAD_ROWS
    k3s, v3s = older_block(2)
    chains = range(2 * SB_SUB)
    raw = [_dot_nt(q_heads[n], k2s[n // 2]) for n in chains]
    z3 = [_dot_nt(q_heads[n][:hr], k3s[n // 2]) for n in chains]
    zs = [jnp.concatenate([raw[n][:, :blk], jnp.where(causal, raw[n][:, blk:], SB_NEG)], axis=1) for n in chains]
    sums = [suffix_sums(_softplus(zs[n]), suffix2x2) for n in chains]
    s3 = [suffix_sums(_softplus(z3[n]), suffix1x2) for n in chains]
    probs = [jnp.exp(zs[n] - sums[n]).astype(BF16) for n in chains]
    carries = [sums[n][:, 0:1] for n in chains]
    p3 = [jnp.exp(z3[n] - carries[n][:hr] - s3[n]).astype(BF16) for n in chains]
    accs = [_dot(probs[n], v2s[n // 2]) for n in chains]
    accs = [jnp.concatenate([accs[n][:hr] + _dot(p3[n], v3s[n // 2]), accs[n][hr:]], axis=0) for n in chains]
    carries = [jnp.concatenate([carries[n][:hr] + s3[n][:, 0:1], carries[n][hr:]], axis=0) for n in chains]
    head_done = row1 < hr

    def lowest(cs):
        m = cs[0]
        for cc in cs[1:]:
            m = jnp.minimum(m, cc)
        return jnp.min(m)

    def more(st):
        it, cmin = st[0], st[1]
        return jnp.logical_and(it < (step + 1) * SB_SUB - 2, cmin < SB_DONE)

    def body(st):
        it = st[0]
        cs = list(st[2:2 + 2 * SB_SUB])
        acs = list(st[2 + 2 * SB_SUB:])
        kblks, vblks = older_block(2 + it)
        skip = jnp.logical_and(it == 0, head_done)
        zl = [jnp.where(skip, SB_NEG, _dot_nt(q_heads[n], kblks[n // 2])) for n in chains]
        sl = [suffix_sums(_softplus(zl[n]), suffix1x2) for n in chains]
        pl_ = [jnp.exp(zl[n] - cs[n] - sl[n]).astype(BF16) for n in chains]
        acs = [acs[n] + _dot(pl_[n], vblks[n // 2]) for n in chains]
        cs = [cs[n] + sl[n][:, 0:1] for n in chains]
        return (it + 1, lowest(cs), *cs, *acs)

    st = lax.while_loop(more, body, (jnp.int32(0), lowest(carries), *carries, *accs))
    accs = st[2 + 2 * SB_SUB:]
    for j in range(SB_SUB):
        o_ref[j * blk:(j + 1) * blk, :] = jnp.where(lane < SB_DH, accs[2 * j], accs[2 * j + 1]).astype(o_ref.dtype)


def _stick_breaking(qproj, kv, bsz, seq):
    t = qproj.shape[0]
    tq = SB_SUB * SB_BLOCK
    n_q = seq // tq
    pairs = SB_HEADS // 2
    pw = 2 * SB_DH
    return pl.pallas_call(
        _sb_kernel,
        grid=(bsz, pairs, n_q),
        in_specs=[pl.BlockSpec((tq, pw), lambda b, p, i: (b * n_q + i, p)),
                  pl.BlockSpec((seq, pw), lambda b, p, i: (b, p)),
                  pl.BlockSpec((seq, pw), lambda b, p, i: (b, pairs + p))],
        out_specs=pl.BlockSpec((tq, pw), lambda b, p, i: (b * n_q + i, p)),
        out_shape=jax.ShapeDtypeStruct((t, SB_WIDTH), BF16),
        compiler_params=pltpu.CompilerParams(dimension_semantics=("parallel", "parallel", "arbitrary"),
                                             vmem_limit_bytes=VMEM_LIMIT),
        name="stick_breaking",
    )(qproj, kv, kv)


MOE_TM = 1024
MOE_SUB = 320


def _split3(x):
    hi = x.astype(BF16)
    r1 = x - hi.astype(F32)
    mid = r1.astype(BF16)
    return hi, mid, (r1 - mid.astype(F32)).astype(BF16)


def _moe_kernel(h_ref, g_ref, wrh_ref, wrl_ref, br_ref, sel_ref, w13_ref, w2_ref, fg_ref, out_ref,
                xn_sc, cw_sc, mcol_sc, mrow_sc, cnt_sc, acc_sc, tril_sc, triu_sc, *, final_norm):
    tile = pl.program_id(0)
    grp = pl.program_id(1)
    tm = h_ref.shape[0]

    @pl.when(jnp.logical_and(tile == 0, grp == 0))
    def _():
        r = lax.broadcasted_iota(jnp.int32, (tm, tm), 0)
        c = lax.broadcasted_iota(jnp.int32, (tm, tm), 1)
        tril_sc[...] = jnp.where(r > c, 1.0, 0.0).astype(BF16)
        triu_sc[...] = jnp.where(r < c, 1.0, 0.0).astype(BF16)

    @pl.when(grp == 0)
    def _():
        x = h_ref[...]
        xn = x * _rms_scale(x) * g_ref[...]
        xh = xn.astype(BF16)
        xn_sc[...] = xh
        xl = (xn - xh.astype(F32)).astype(BF16)
        logits = _dot(xh, wrh_ref[...]) + _dot(xl, wrh_ref[...]) + _dot(xh, wrl_ref[...]) + br_ref[...]
        lane = lax.broadcasted_iota(jnp.int32, (tm, LANES), 1)
        lane_f = lane.astype(F32)
        big = float(LANES)
        is_g = lane < N_GROUPS
        gl = jnp.where(is_g, logits, -jnp.inf)
        gmax = jnp.max(gl, axis=-1, keepdims=True)
        p_group = 1.0 / jnp.sum(jnp.where(is_g, jnp.exp(logits - gmax), 0.0), axis=-1, keepdims=True)
        gsel = jnp.min(jnp.where(gl == gmax, lane_f, big), axis=-1, keepdims=True)
        e_lane = lane - ROUTER_LANE0
        grp_of_lane = (e_lane // EXPERTS_PER_GROUP).astype(F32)
        in_grp = jnp.logical_and(jnp.logical_and(e_lane >= 0, e_lane < N_EXPERTS), grp_of_lane == gsel)
        el = jnp.where(in_grp, logits, -jnp.inf)
        emax = jnp.max(el, axis=-1, keepdims=True)
        ee = jnp.where(in_grp, jnp.exp(logits - emax), 0.0)
        prob = ee / jnp.sum(ee, axis=-1, keepdims=True)
        pm = jnp.where(in_grp, prob, -1.0)
        p1 = jnp.max(pm, axis=-1, keepdims=True)
        i1 = jnp.min(jnp.where(pm == p1, lane_f, big), axis=-1, keepdims=True)
        pm2 = jnp.where(lane_f == i1, -1.0, pm)
        p2 = jnp.max(pm2, axis=-1, keepdims=True)
        i2 = jnp.min(jnp.where(pm2 == p2, lane_f, big), axis=-1, keepdims=True)
        sel = jnp.logical_or(lane_f == i1, lane_f == i2)
        comb = jnp.where(sel, p_group * (prob / (p1 + p2)), 0.0)
        ch, cm, cl = _split3(comb)
        lane_sel = sel_ref[...]
        comb4 = _dot(ch, lane_sel) + _dot(cm, lane_sel) + _dot(cl, lane_sel)
        c4h, c4m, c4l = _split3(comb4)
        cw_sc[:, 0:LANES] = c4h
        cw_sc[:, LANES:2 * LANES] = c4m
        cw_sc[:, 2 * LANES:3 * LANES] = c4l
        onehot = jnp.where(lane_f == gsel, 1.0, 0.0)
        rank_col = jnp.sum(onehot * _dot(tril_sc[...], onehot.astype(BF16)), axis=-1, keepdims=True)
        mcol_sc[...] = jnp.where(lane == 0, gsel, jnp.where(lane == 1, rank_col, 0.0))
        cnt_sc[...] = jnp.broadcast_to(jnp.sum(onehot, axis=0, keepdims=True), cnt_sc.shape)
        oht = onehot.T
        cumt = _dot(oht.astype(BF16), triu_sc[...])
        sub = lax.broadcasted_iota(jnp.int32, (LANES, tm), 0)
        rank_row = jnp.sum(oht * cumt, axis=0, keepdims=True)
        grp_row = jnp.sum(oht * sub.astype(F32), axis=0, keepdims=True)
        srow = lax.broadcasted_iota(jnp.int32, mrow_sc.shape, 0)
        mrow_sc[...] = jnp.where(srow == 0, grp_row, jnp.where(srow == 1, rank_row, 0.0))
        acc_sc[...] = jnp.zeros(acc_sc.shape, F32)

    grp_f = grp.astype(F32)
    cnt_lane = lax.broadcasted_iota(jnp.int32, (1, LANES), 1)
    n_rows = jnp.sum(jnp.where(cnt_lane == grp, cnt_sc[0:1, :], 0.0)).astype(jnp.int32)
    rel_row = jnp.where(mrow_sc[0:1, :] == grp_f, mrow_sc[1:2, :], -1.0)
    mcol = mcol_sc[...]
    rel_col = jnp.where(mcol[:, 0:1] == grp_f, mcol[:, 1:2], -1.0)
    r_iota = lax.broadcasted_iota(jnp.int32, (MOE_SUB, tm), 0).astype(F32)
    c_iota = lax.broadcasted_iota(jnp.int32, (tm, MOE_SUB), 1).astype(F32)

    def sub_tile(s, carry):
        base = (s * MOE_SUB).astype(F32)
        gather = jnp.where(r_iota + base == rel_row, 1.0, 0.0).astype(BF16)
        xs = _dot(gather, xn_sc[...]).astype(BF16)
        cw3 = _dot(gather, cw_sc[...])
        cw = cw3[:, 0:LANES] + cw3[:, LANES:2 * LANES] + cw3[:, 2 * LANES:3 * LANES]
        y = jnp.zeros((MOE_SUB, out_ref.shape[1]), F32)
        for e in range(EXPERTS_PER_GROUP):
            hcat = _dot(xs, w13_ref[e])
            hg = hcat[:, :EXPERT_FF]
            hid = hg * _sigmoid(hg) * hcat[:, EXPERT_FF:]
            y = y + _dot((hid * cw[:, e:e + 1]).astype(BF16), w2_ref[e])
        scatter = jnp.where(c_iota + base == rel_col, 1.0, 0.0).astype(BF16)
        acc_sc[...] += _dot(scatter, y.astype(BF16))
        return carry

    lax.fori_loop(0, (n_rows + MOE_SUB - 1) // MOE_SUB, sub_tile, 0)

    @pl.when(grp == N_GROUPS - 1)
    def _():
        y = h_ref[...] + acc_sc[...]
        if final_norm:
            y = y * _rms_scale(y) * fg_ref[...]
        out_ref[...] = y


def _moe(h2d, layer, ffn_g, w_group, b_group, w_router, b_router, w13_all, w2_all, final_g, final_norm):
    t, d = h2d.shape
    tm = min(MOE_TM, t)
    pad = LANES - N_GROUPS - N_EXPERTS
    wr = jnp.concatenate([w_group, w_router, jnp.zeros((d, pad), F32)], axis=1).astype(F32)
    wr_hi = wr.astype(BF16)
    wr_lo = (wr - wr_hi.astype(F32)).astype(BF16)
    br = jnp.concatenate([b_group, b_router, jnp.zeros((pad,), F32)]).astype(F32).reshape(1, LANES)
    src = jnp.arange(LANES)[:, None]
    dst = jnp.arange(LANES)[None, :]
    is_expert = jnp.logical_and(src >= ROUTER_LANE0, src < ROUTER_LANE0 + N_EXPERTS)
    lane_sel = jnp.logical_and(is_expert, (src - ROUTER_LANE0) % EXPERTS_PER_GROUP == dst).astype(BF16)
    const = lambda i, g: (0, 0)
    return pl.pallas_call(
        functools.partial(_moe_kernel, final_norm=final_norm),
        grid=(t // tm, N_GROUPS),
        in_specs=[pl.BlockSpec((tm, d), lambda i, g: (i, 0)),
                  pl.BlockSpec((1, d), const),
                  pl.BlockSpec((d, LANES), const),
                  pl.BlockSpec((d, LANES), const),
                  pl.BlockSpec((1, LANES), const),
                  pl.BlockSpec((LANES, LANES), const),
                  pl.BlockSpec((None, EXPERTS_PER_GROUP, d, 2 * EXPERT_FF), lambda i, g: (layer, g, 0, 0)),
                  pl.BlockSpec((None, EXPERTS_PER_GROUP, EXPERT_FF, d), lambda i, g: (layer, g, 0, 0)),
                  pl.BlockSpec((1, d), const)],
        out_specs=pl.BlockSpec((tm, d), lambda i, g: (i, 0)),
        out_shape=jax.ShapeDtypeStruct((t, d), F32),
        scratch_shapes=[pltpu.VMEM((tm, d), BF16),
                        pltpu.VMEM((tm, 3 * LANES), BF16),
                        pltpu.VMEM((tm, LANES), F32),
                        pltpu.VMEM((8, tm), F32),
                        pltpu.VMEM((8, LANES), F32),
                        pltpu.VMEM((tm, d), F32),
                        pltpu.VMEM((tm, tm), BF16), pltpu.VMEM((tm, tm), BF16)],
        compiler_params=pltpu.CompilerParams(dimension_semantics=("arbitrary", "arbitrary"),
                                             vmem_limit_bytes=VMEM_LIMIT),
        name="moe",
    )(h2d, ffn_g.reshape(1, d), wr_hi, wr_lo, br, lane_sel, w13_all, w2_all, final_g.reshape(1, d))


def kernel(x, mem, a_norm, a_w_in, a_conv, a_log, a_dt_bias, a_out_gain, a_w_out, kv_norm, w_kv, b_norm, b_w_in,
           b_w_out, mem_norm, w_mem_kv, ffn_norm, w_group, b_group, w_router, b_router, w1, w3, w2, final_norm):
    bsz, seq, d = x.shape
    assert seq % (SB_SUB * SB_BLOCK) == 0 and seq % GDN_TOK == 0 and seq % MIX_TS == 0, seq
    m_len = mem.shape[1]
    depth = mem_norm.shape[0]
    n_a = a_norm.shape[0]
    h = x.reshape(bsz * seq, d)
    mem2d = mem.reshape(bsz * m_len, d)
    w13_all = jnp.concatenate([w1.astype(BF16), w3.astype(BF16)], axis=-1)
    w2_all = w2.astype(BF16)
    kv = None
    for l in range(depth):
        mkv = _norm_matmul(mem2d, mem_norm[l], w_mem_kv[l], F32, PROJ_TM)
        if l < n_a:
            w_in = a_w_in[l]
            g4 = 4 * GDN_WIDTH
            w_perm = jnp.concatenate([w_in[:, :g4], w_in[:, g4 + 2 * GDN_HEADS:], w_in[:, g4:g4 + 2 * GDN_HEADS],
                                      jnp.zeros((d, LANES - 2 * GDN_HEADS), w_in.dtype)], axis=1)
            proj = _norm_matmul(h, a_norm[l], w_perm, F32, PROJ_TM)
            o = _gdn(proj, a_conv[l], a_log[l], a_dt_bias[l], a_out_gain[l], bsz, seq)
            h = _mix_out(h, o, proj, g4 // MEM_WIDTH, mkv, a_w_out[l], bsz, seq, MIX_TS)
        else:
            lb = l - n_a
            if l == n_a:
                kv, proj = _norm_matmul2(h, kv_norm, w_kv, b_norm[lb], b_w_in[lb], BF16, PROJ_TM)
            else:
                proj = _norm_matmul(h, b_norm[lb], b_w_in[lb], BF16, PROJ_TM)
            o = _stick_breaking(proj, kv, bsz, seq)
            h = _mix_out(h, o, proj, SB_WIDTH // MEM_WIDTH, mkv, b_w_out[lb], bsz, seq, MIX_TS)
        h = _moe(h, l, ffn_norm[l], w_group[l], b_group[l], w_router[l], b_router[l], w13_all, w2_all,
                 final_norm, l == depth - 1)
    return h.reshape(bsz, seq, d)
```

```python
import functools

import jax
import jax.numpy as jnp
from jax import lax
from jax.experimental import pallas as pl
from jax.experimental.pallas import tpu as pltpu

EPS = 1e-6
GDN_HEADS = 6
GDN_D = 128
GDN_WIDTH = GDN_HEADS * GDN_D
CONV_K = 4
CHUNK = 64
SB_HEADS = 12
SB_DH = 64
SB_WIDTH = SB_HEADS * SB_DH
SB_BLOCK = 128
MEM_HEADS = 4
MEM_DH = 64
MEM_WIDTH = MEM_HEADS * MEM_DH
N_GROUPS = 4
EXPERTS_PER_GROUP = 4
N_EXPERTS = N_GROUPS * EXPERTS_PER_GROUP
EXPERT_FF = 256
LANES = 128
A_IN_PAD = 4 * GDN_WIDTH + MEM_WIDTH + LANES
ROUTER_LANE0 = N_GROUPS
VMEM_LIMIT = 48 * 1024 * 1024
PROJ_TM = 512
MIX_TS = 256

F32 = jnp.float32
BF16 = jnp.bfloat16


def _dot(a, b):
    return jnp.dot(a, b, preferred_element_type=F32)


def _dot_nt(a, b):
    return lax.dot_general(a, b, (((1,), (1,)), ((), ())), preferred_element_type=F32)


def _dot_f32(a, b):
    return jnp.dot(a, b, preferred_element_type=F32, precision=lax.Precision.HIGHEST)


def _sigmoid(x):
    return 1.0 / (1.0 + jnp.exp(-x))


def _softplus(x):
    return jnp.maximum(x, 0.0) + jnp.log(1.0 + jnp.exp(-jnp.abs(x)))


def _rms_scale(x):
    return lax.rsqrt(jnp.mean(x * x, axis=-1, keepdims=True) + EPS)


def _norm_matmul_kernel(x_ref, g_ref, w_ref, o_ref, *, col_chunk):
    x = x_ref[...]
    xn = (x * _rms_scale(x) * g_ref[...]).astype(BF16)
    n = o_ref.shape[-1]
    for c0 in range(0, n, col_chunk):
        c1 = min(n, c0 + col_chunk)
        o_ref[:, c0:c1] = _dot(xn, w_ref[:, c0:c1]).astype(o_ref.dtype)


def _norm_matmul(x2d, g, w, out_dtype, tm):
    t, d = x2d.shape
    n = w.shape[1]
    tm = min(tm, t)
    return pl.pallas_call(
        functools.partial(_norm_matmul_kernel, col_chunk=4 * LANES),
        grid=(t // tm,),
        in_specs=[pl.BlockSpec((tm, d), lambda i: (i, 0)),
                  pl.BlockSpec((1, d), lambda i: (0, 0)),
                  pl.BlockSpec((d, n), lambda i: (0, 0))],
        out_specs=pl.BlockSpec((tm, n), lambda i: (i, 0)),
        out_shape=jax.ShapeDtypeStruct((t, n), out_dtype),
        compiler_params=pltpu.CompilerParams(dimension_semantics=("parallel",), vmem_limit_bytes=VMEM_LIMIT),
        name="norm_matmul",
    )(x2d, g.reshape(1, d), w.astype(BF16))


def _norm_matmul2_kernel(x_ref, ga_ref, wa_ref, gb_ref, wb_ref, oa_ref, ob_ref, *, col_chunk):
    x = x_ref[...]
    xs = x * _rms_scale(x)
    for g_ref, w_ref, o_ref in ((ga_ref, wa_ref, oa_ref), (gb_ref, wb_ref, ob_ref)):
        xn = (xs * g_ref[...]).astype(BF16)
        n = o_ref.shape[-1]
        for c0 in range(0, n, col_chunk):
            c1 = min(n, c0 + col_chunk)
            o_ref[:, c0:c1] = _dot(xn, w_ref[:, c0:c1]).astype(o_ref.dtype)


def _norm_matmul2(x2d, ga, wa, gb, wb, out_dtype, tm):
    t, d = x2d.shape
    na, nb = wa.shape[1], wb.shape[1]
    tm = min(tm, t)
    const = lambda i: (0, 0)
    return pl.pallas_call(
        functools.partial(_norm_matmul2_kernel, col_chunk=4 * LANES),
        grid=(t // tm,),
        in_specs=[pl.BlockSpec((tm, d), lambda i: (i, 0)),
                  pl.BlockSpec((1, d), const), pl.BlockSpec((d, na), const),
                  pl.BlockSpec((1, d), const), pl.BlockSpec((d, nb), const)],
        out_specs=[pl.BlockSpec((tm, na), lambda i: (i, 0)), pl.BlockSpec((tm, nb), lambda i: (i, 0))],
        out_shape=[jax.ShapeDtypeStruct((t, na), out_dtype), jax.ShapeDtypeStruct((t, nb), out_dtype)],
        compiler_params=pltpu.CompilerParams(dimension_semantics=("parallel",), vmem_limit_bytes=VMEM_LIMIT),
        name="norm_matmul2",
    )(x2d, ga.reshape(1, d), wa.astype(BF16), gb.reshape(1, d), wb.astype(BF16))


GDN_STEP_CHUNKS = 8
GDN_TOK = CHUNK * GDN_STEP_CHUNKS
CONV_TAIL = 8


def _gdn_kernel(qkv_ref, gate_ref, ba_ref, convw_ref, alog_ref, dtb_ref, ogain_ref, o_ref, xbuf, state):
    c = CHUNK
    tok = GDN_TOK

    @pl.when(pl.program_id(1) == 0)
    def _():
        xbuf[0:CONV_TAIL, :] = jnp.zeros((CONV_TAIL, xbuf.shape[1]), F32)
        state[...] = jnp.zeros(state.shape, F32)

    xbuf[CONV_TAIL:CONV_TAIL + tok, :] = qkv_ref[...]

    def conv_act(lo):
        win = xbuf[:, lo:lo + GDN_D]
        acc = win[CONV_TAIL:] * convw_ref[CONV_K - 1:CONV_K, lo:lo + GDN_D]
        for j in range(CONV_K - 1):
            shifted = pltpu.roll(win, CONV_K - 1 - j, axis=0)[CONV_TAIL:]
            acc = acc + shifted * convw_ref[j:j + 1, lo:lo + GDN_D]
        return acc * _sigmoid(acc)

    ba = ba_ref[...]
    beta_all = _sigmoid(ba)
    g_all = -jnp.exp(alog_ref[...]) * _softplus(ba + dtb_ref[...])
    trow = lax.broadcasted_iota(jnp.int32, (tok, tok), 0)
    tcol = lax.broadcasted_iota(jnp.int32, (tok, tok), 1)
    same_chunk = (trow // c) == (tcol // c)
    lower = jnp.where(jnp.logical_and(same_chunk, trow >= tcol), 1.0, 0.0)
    upper = jnp.where(jnp.logical_and(same_chunk, trow <= tcol), 1.0, 0.0)
    gc_all = _dot_f32(lower, g_all)
    gct_all = _dot_f32(g_all.T, upper)
    row = lax.broadcasted_iota(jnp.int32, (c, c), 0)
    col = lax.broadcasted_iota(jnp.int32, (c, c), 1)
    incl = row >= col
    strict = row > col
    eye = (row == col).astype(F32)
    ogain = ogain_ref[...]

    heads = range(GDN_HEADS)
    chunks = range(GDN_STEP_CHUNKS)
    items = [(ci, h) for ci in chunks for h in heads]

    q_full, k_full, v_full = [], [], []
    for h in heads:
        lo = h * GDN_D
        qf = conv_act(lo)
        kf = conv_act(GDN_WIDTH + lo)
        q_full.append(qf * lax.rsqrt(jnp.sum(qf * qf, axis=-1, keepdims=True) + EPS) * (GDN_D ** -0.5))
        k_full.append(kf * lax.rsqrt(jnp.sum(kf * kf, axis=-1, keepdims=True) + EPS))
        v_full.append(conv_act(2 * GDN_WIDTH + lo))

    pre = {}
    for ci, h in items:
        r0 = ci * c
        gl = GDN_HEADS + h
        qh = q_full[h][r0:r0 + c]
        kh = k_full[h][r0:r0 + c]
        vh = v_full[h][r0:r0 + c]
        beta = beta_all[r0:r0 + c, h:h + 1]
        gcol = gc_all[r0:r0 + c, gl:gl + 1]
        grow = gct_all[gl:gl + 1, r0:r0 + c]
        glast = gc_all[r0 + c - 1:r0 + c, gl:gl + 1]
        decay = jnp.where(incl, jnp.exp(jnp.where(incl, gcol - grow, 0.0)), 0.0)
        kb = kh * beta
        khb = kh.astype(BF16)
        egc = jnp.exp(gcol)
        pre[ci, h] = dict(
            p=-jnp.where(strict, _dot_nt(kb.astype(BF16), khb) * decay, 0.0),
            att=jnp.where(incl, _dot_nt(qh.astype(BF16), khb) * decay, 0.0).astype(BF16),
            rhs=jnp.concatenate([vh * beta, kb * egc], axis=1).astype(BF16),
            qg=qh * egc,
            ktail_t=(kh * jnp.exp(glast - gcol)).T.astype(BF16),
            sdecay=jnp.exp(glast))

    tinv = {it: eye + pre[it]["p"] for it in items}
    pw = {it: pre[it]["p"] for it in items}
    for _ in range(5):
        for it in items:
            pb = pw[it].astype(BF16)
            pw[it] = _dot(pb, pb)
        for it in items:
            tinv[it] = tinv[it] + _dot(tinv[it].astype(BF16), pw[it].astype(BF16))
    sol = {it: _dot(tinv[it].astype(BF16), pre[it]["rhs"]) for it in items}

    drow = lax.broadcasted_iota(jnp.int32, (GDN_D, GDN_D), 0)
    dcol = lax.broadcasted_iota(jnp.int32, (GDN_D, GDN_D), 1)
    eye_d = (drow == dcol).astype(F32)
    lin = {}
    for it in items:
        pr = pre[it]
        solb = sol[it].astype(BF16)
        att_sol = _dot(pr["att"], solb)
        kt_sol = _dot(pr["ktail_t"], solb)
        lin[it] = ((pr["qg"] - att_sol[:, GDN_D:]).astype(BF16), att_sol[:, :GDN_D],
                   (eye_d * pr["sdecay"] - kt_sol[:, GDN_D:]).astype(BF16), kt_sol[:, :GDN_D])
    s_cur = [state[h] for h in heads]
    for ci in chunks:
        r0 = ci * c
        for h in heads:
            q_lin, o_const, s_lin, s_const = lin[ci, h]
            lo = h * GDN_D
            sb = s_cur[h].astype(BF16)
            o = _dot(q_lin, sb) + o_const
            s_cur[h] = _dot(s_lin, sb) + s_const
            gate = gate_ref[r0:r0 + c, lo:lo + GDN_D]
            o = o * _rms_scale(o) * ogain * (gate * _sigmoid(gate))
            o_ref[r0:r0 + c, lo:lo + GDN_D] = o.astype(o_ref.dtype)
    for h in heads:
        state[h] = s_cur[h]
    xbuf[0:CONV_TAIL, :] = xbuf[tok:tok + CONV_TAIL, :]


def _gdn(proj, conv_w, a_log, dt_bias, o_gain, bsz, seq):
    t = proj.shape[0]
    n_s = seq // GDN_TOK
    qkv_w = 3 * GDN_WIDTH
    pad = jnp.zeros((GDN_HEADS,), F32)
    alog_row = jnp.concatenate([pad, a_log.astype(F32), jnp.zeros((LANES - 2 * GDN_HEADS,), F32)]).reshape(1, LANES)
    dtb_row = jnp.concatenate([pad, dt_bias.astype(F32), jnp.zeros((LANES - 2 * GDN_HEADS,), F32)]).reshape(1, LANES)
    return pl.pallas_call(
        _gdn_kernel,
        grid=(bsz, n_s),
        in_specs=[pl.BlockSpec((GDN_TOK, qkv_w), lambda b, s: (b * n_s + s, 0)),
                  pl.BlockSpec((GDN_TOK, GDN_WIDTH), lambda b, s: (b * n_s + s, qkv_w // GDN_WIDTH)),
                  pl.BlockSpec((GDN_TOK, LANES), lambda b, s: (b * n_s + s, (A_IN_PAD - LANES) // LANES)),
                  pl.BlockSpec((CONV_K, qkv_w), lambda b, s: (0, 0)),
                  pl.BlockSpec((1, LANES), lambda b, s: (0, 0)),
                  pl.BlockSpec((1, LANES), lambda b, s: (0, 0)),
                  pl.BlockSpec((1, GDN_D), lambda b, s: (0, 0))],
        out_specs=pl.BlockSpec((GDN_TOK, GDN_WIDTH), lambda b, s: (b * n_s + s, 0)),
        out_shape=jax.ShapeDtypeStruct((t, GDN_WIDTH), BF16),
        scratch_shapes=[pltpu.VMEM((GDN_TOK + CONV_TAIL, qkv_w), F32),
                        pltpu.VMEM((GDN_HEADS, GDN_D, GDN_D), F32)],
        compiler_params=pltpu.CompilerParams(dimension_semantics=("parallel", "arbitrary"),
                                             vmem_limit_bytes=VMEM_LIMIT),
        name="gdn",
    )(proj, proj, proj, conv_w.astype(F32), alog_row, dtb_row, o_gain.astype(F32).reshape(1, GDN_D))


def _mix_out_kernel(h_ref, o_ref, mq_ref, mk_ref, mv_ref, wo_ref, wm_ref, out_ref):
    mq = (mq_ref[...] * (MEM_DH ** -0.5)).astype(BF16)
    mk = mk_ref[...]
    mv = mv_ref[...]
    head = lax.broadcasted_iota(jnp.int32, mk.shape, 1) // MEM_DH
    heads = range(MEM_HEADS)
    k_heads = [jnp.where(head == hh, mk, 0.0).astype(BF16) for hh in heads]
    v_heads = [jnp.where(head == hh, mv, 0.0).astype(BF16) for hh in heads]
    scores = [_dot_nt(mq, k_heads[hh]) for hh in heads]
    probs = []
    for sc in scores:
        p = jnp.exp(sc - jnp.max(sc, axis=-1, keepdims=True))
        probs.append((p * (1.0 / jnp.sum(p, axis=-1, keepdims=True))).astype(BF16))
    m = _dot(probs[0], v_heads[0])
    for hh in heads[1:]:
        m = m + _dot(probs[hh], v_heads[hh])
    y = _dot(o_ref[...].astype(BF16), wo_ref[...]) + _dot(m.astype(BF16), wm_ref[...])
    out_ref[...] = h_ref[...] + y


def _mix_out(h2d, o, proj, mq_block, mkv, w_out, bsz, seq, ts):
    t, d = h2d.shape
    ts = min(ts, seq)
    n_s = seq // ts
    m_len = mkv.shape[0] // bsz
    ow = o.shape[1]
    return pl.pallas_call(
        _mix_out_kernel,
        grid=(bsz, n_s),
        in_specs=[pl.BlockSpec((ts, d), lambda b, s: (b * n_s + s, 0)),
                  pl.BlockSpec((ts, ow), lambda b, s: (b * n_s + s, 0)),
                  pl.BlockSpec((ts, MEM_WIDTH), lambda b, s: (b * n_s + s, mq_block)),
                  pl.BlockSpec((m_len, MEM_WIDTH), lambda b, s: (b, 0)),
                  pl.BlockSpec((m_len, MEM_WIDTH), lambda b, s: (b, 1)),
                  pl.BlockSpec((ow, d), lambda b, s: (0, 0)),
                  pl.BlockSpec((MEM_WIDTH, d), lambda b, s: (0, 0))],
        out_specs=pl.BlockSpec((ts, d), lambda b, s: (b * n_s + s, 0)),
        out_shape=jax.ShapeDtypeStruct((t, d), F32),
        compiler_params=pltpu.CompilerParams(dimension_semantics=("parallel", "parallel"),
                                             vmem_limit_bytes=VMEM_LIMIT),
        name="mix_out",
    )(h2d, o, proj, mkv, mkv, w_out[:ow].astype(BF16), w_out[ow:].astype(BF16))


SB_SUB = 8
SB_NEG = -1e30
SB_DONE = 88.0
SB_HEAD_ROWS = 16


def _split_bf16(x):
    hi = pltpu.bitcast(pltpu.bitcast(x, jnp.uint32) & jnp.uint32(0xFFFF0000), F32)
    return hi.astype(BF16), (x - hi).astype(BF16)


def _sb_kernel(q_ref, k_ref, v_ref, o_ref):
    blk = SB_BLOCK
    step = pl.program_id(2)
    lane = lax.broadcasted_iota(jnp.int32, (blk, 2 * SB_DH), 1)
    row1 = lax.broadcasted_iota(jnp.int32, (blk, blk), 0)
    col1 = lax.broadcasted_iota(jnp.int32, (blk, blk), 1)
    causal = col1 < row1
    srow2 = lax.broadcasted_iota(jnp.int32, (2 * blk, 2 * blk), 0)
    scol2 = lax.broadcasted_iota(jnp.int32, (2 * blk, 2 * blk), 1)
    suffix2 = (srow2 >= scol2).astype(BF16)
    suffix1 = suffix2[:blk, :blk]
    suffix2x2 = jnp.concatenate([suffix2, suffix2], axis=0)
    suffix1x2 = jnp.concatenate([suffix1, suffix1], axis=0)

    def suffix_sums(sp, suffix_x2):
        return _dot(jnp.concatenate(_split_bf16(sp), axis=1), suffix_x2)

    q_heads, k2s, v2s = [], [], []
    for j in range(SB_SUB):
        i = step * SB_SUB + j
        q = q_ref[j * blk:(j + 1) * blk, :] * (SB_DH ** -0.5)
        q_heads.append(jnp.where(lane < SB_DH, q, 0.0).astype(BF16))
        q_heads.append(jnp.where(lane >= SB_DH, q, 0.0).astype(BF16))
        prev = pl.multiple_of(jnp.maximum(i - 1, 0) * blk, blk)
        diag = pl.multiple_of(i * blk, blk)
        k2s.append(jnp.concatenate([k_ref[pl.ds(prev, blk), :], k_ref[pl.ds(diag, blk), :]], axis=0))
        has_prev = jnp.where(i > 0, 1.0, 0.0).astype(BF16)
        v2s.append(jnp.concatenate([v_ref[pl.ds(prev, blk), :] * has_prev, v_ref[pl.ds(diag, blk), :]], axis=0))
    def older_block(offset):
        kblks, vblks = [], []
        for j in range(SB_SUB):
            kb = step * SB_SUB + j - offset
            start = pl.multiple_of(jnp.maximum(kb, 0) * blk, blk)
            kblks.append(k_ref[pl.ds(start, blk), :])
            vblks.append(v_ref[pl.ds(start, blk), :] * jnp.where(kb >= 0, 1.0, 0.0).astype(BF16))
        return kblks, vblks

    hr = SB_HEAD_ROWS
    k3s, v3s = older_block(2)
    chains = range(2 * SB_SUB)
    raw = [_dot_nt(q_heads[n], k2s[n // 2]) for n in chains]
    z3 = [_dot_nt(q_heads[n][:hr], k3s[n // 2]) for n in chains]
    zs = [jnp.concatenate([raw[n][:, :blk], jnp.where(causal, raw[n][:, blk:], SB_NEG)], axis=1) for n in chains]
    sums = [suffix_sums(_softplus(zs[n]), suffix2x2) for n in chains]
    s3 = [suffix_sums(_softplus(z3[n]), suffix1x2) for n in chains]
    probs = [jnp.exp(zs[n] - sums[n]).astype(BF16) for n in chains]
    carries = [sums[n][:, 0:1] for n in chains]
    p3 = [jnp.exp(z3[n] - carries[n][:hr] - s3[n]).astype(BF16) for n in chains]
    accs = [_dot(probs[n], v2s[n // 2]) for n in chains]
    accs = [jnp.concatenate([accs[n][:hr] + _dot(p3[n], v3s[n // 2]), accs[n][hr:]], axis=0) for n in chains]
    carries = [jnp.concatenate([carries[n][:hr] + s3[n][:, 0:1], carries[n][hr:]], axis=0) for n in chains]
    head_done = row1 < hr

    def lowest(cs):
        m = cs[0]
        for cc in cs[1:]:
            m = jnp.minimum(m, cc)
        return jnp.min(m)

    def more(st):
        it, cmin = st[0], st[1]
        return jnp.logical_and(it < (step + 1) * SB_SUB - 2, cmin < SB_DONE)

    def body(st):
        it = st[0]
        cs = list(st[2:2 + 2 * SB_SUB])
        acs = list(st[2 + 2 * SB_SUB:])
        kblks, vblks = older_block(2 + it)
        skip = jnp.logical_and(it == 0, head_done)
        zl = [jnp.where(skip, SB_NEG, _dot_nt(q_heads[n], kblks[n // 2])) for n in chains]
        sl = [suffix_sums(_softplus(zl[n]), suffix1x2) for n in chains]
        pl_ = [jnp.exp(zl[n] - cs[n] - sl[n]).astype(BF16) for n in chains]
        acs = [acs[n] + _dot(pl_[n], vblks[n // 2]) for n in chains]
        cs = [cs[n] + sl[n][:, 0:1] for n in chains]
        return (it + 1, lowest(cs), *cs, *acs)

    st = lax.while_loop(more, body, (jnp.int32(0), lowest(carries), *carries, *accs))
    accs = st[2 + 2 * SB_SUB:]
    for j in range(SB_SUB):
        o_ref[j * blk:(j + 1) * blk, :] = jnp.where(lane < SB_DH, accs[2 * j], accs[2 * j + 1]).astype(o_ref.dtype)


def _stick_breaking(qproj, kv, bsz, seq):
    t = qproj.shape[0]
    tq = SB_SUB * SB_BLOCK
    n_q = seq // tq
    pairs = SB_HEADS // 2
    pw = 2 * SB_DH
    return pl.pallas_call(
        _sb_kernel,
        grid=(bsz, pairs, n_q),
        in_specs=[pl.BlockSpec((tq, pw), lambda b, p, i: (b * n_q + i, p)),
                  pl.BlockSpec((seq, pw), lambda b, p, i: (b, p)),
                  pl.BlockSpec((seq, pw), lambda b, p, i: (b, pairs + p))],
        out_specs=pl.BlockSpec((tq, pw), lambda b, p, i: (b * n_q + i, p)),
        out_shape=jax.ShapeDtypeStruct((t, SB_WIDTH), BF16),
        compiler_params=pltpu.CompilerParams(dimension_semantics=("parallel", "parallel", "arbitrary"),
                                             vmem_limit_bytes=VMEM_LIMIT),
        name="stick_breaking",
    )(qproj, kv, kv)


MOE_TM = 1024
MOE_SUB = 320


def _split3(x):
    hi = x.astype(BF16)
    r1 = x - hi.astype(F32)
    mid = r1.astype(BF16)
    return hi, mid, (r1 - mid.astype(F32)).astype(BF16)


def _moe_kernel(h_ref, g_ref, wrh_ref, wrl_ref, br_ref, sel_ref, w13_ref, w2_ref, fg_ref, out_ref,
                xn_sc, cw_sc, mcol_sc, mrow_sc, cnt_sc, acc_sc, tril_sc, triu_sc, *, final_norm):
    tile = pl.program_id(0)
    grp = pl.program_id(1)
    tm = h_ref.shape[0]

    @pl.when(jnp.logical_and(tile == 0, grp == 0))
    def _():
        r = lax.broadcasted_iota(jnp.int32, (tm, tm), 0)
        c = lax.broadcasted_iota(jnp.int32, (tm, tm), 1)
        tril_sc[...] = jnp.where(r > c, 1.0, 0.0).astype(BF16)
        triu_sc[...] = jnp.where(r < c, 1.0, 0.0).astype(BF16)

    @pl.when(grp == 0)
    def _():
        x = h_ref[...]
        xn = x * _rms_scale(x) * g_ref[...]
        xh = xn.astype(BF16)
        xn_sc[...] = xh
        xl = (xn - xh.astype(F32)).astype(BF16)
        logits = _dot(xh, wrh_ref[...]) + _dot(xl, wrh_ref[...]) + _dot(xh, wrl_ref[...]) + br_ref[...]
        lane = lax.broadcasted_iota(jnp.int32, (tm, LANES), 1)
        lane_f = lane.astype(F32)
        big = float(LANES)
        is_g = lane < N_GROUPS
        gl = jnp.where(is_g, logits, -jnp.inf)
        gmax = jnp.max(gl, axis=-1, keepdims=True)
        p_group = 1.0 / jnp.sum(jnp.where(is_g, jnp.exp(logits - gmax), 0.0), axis=-1, keepdims=True)
        gsel = jnp.min(jnp.where(gl == gmax, lane_f, big), axis=-1, keepdims=True)
        e_lane = lane - ROUTER_LANE0
        grp_of_lane = (e_lane // EXPERTS_PER_GROUP).astype(F32)
        in_grp = jnp.logical_and(jnp.logical_and(e_lane >= 0, e_lane < N_EXPERTS), grp_of_lane == gsel)
        el = jnp.where(in_grp, logits, -jnp.inf)
        emax = jnp.max(el, axis=-1, keepdims=True)
        ee = jnp.where(in_grp, jnp.exp(logits - emax), 0.0)
        prob = ee / jnp.sum(ee, axis=-1, keepdims=True)
        pm = jnp.where(in_grp, prob, -1.0)
        p1 = jnp.max(pm, axis=-1, keepdims=True)
        i1 = jnp.min(jnp.where(pm == p1, lane_f, big), axis=-1, keepdims=True)
        pm2 = jnp.where(lane_f == i1, -1.0, pm)
        p2 = jnp.max(pm2, axis=-1, keepdims=True)
        i2 = jnp.min(jnp.where(pm2 == p2, lane_f, big), axis=-1, keepdims=True)
        sel = jnp.logical_or(lane_f == i1, lane_f == i2)
        comb = jnp.where(sel, p_group * (prob / (p1 + p2)), 0.0)
        ch, cm, cl = _split3(comb)
        lane_sel = sel_ref[...]
        comb4 = _dot(ch, lane_sel) + _dot(cm, lane_sel) + _dot(cl, lane_sel)
        c4h, c4m, c4l = _split3(comb4)
        cw_sc[:, 0:LANES] = c4h
        cw_sc[:, LANES:2 * LANES] = c4m
        cw_sc[:, 2 * LANES:3 * LANES] = c4l
        onehot = jnp.where(lane_f == gsel, 1.0, 0.0)
        rank_col = jnp.sum(onehot * _dot(tril_sc[...], onehot.astype(BF16)), axis=-1, keepdims=True)
        mcol_sc[...] = jnp.where(lane == 0, gsel, jnp.where(lane == 1, rank_col, 0.0))
        cnt_sc[...] = jnp.broadcast_to(jnp.sum(onehot, axis=0, keepdims=True), cnt_sc.shape)
        oht = onehot.T
        cumt = _dot(oht.astype(BF16), triu_sc[...])
        sub = lax.broadcasted_iota(jnp.int32, (LANES, tm), 0)
        rank_row = jnp.sum(oht * cumt, axis=0, keepdims=True)
        grp_row = jnp.sum(oht * sub.astype(F32), axis=0, keepdims=True)
        srow = lax.broadcasted_iota(jnp.int32, mrow_sc.shape, 0)
        mrow_sc[...] = jnp.where(srow == 0, grp_row, jnp.where(srow == 1, rank_row, 0.0))
        acc_sc[...] = jnp.zeros(acc_sc.shape, F32)

    grp_f = grp.astype(F32)
    cnt_lane = lax.broadcasted_iota(jnp.int32, (1, LANES), 1)
    n_rows = jnp.sum(jnp.where(cnt_lane == grp, cnt_sc[0:1, :], 0.0)).astype(jnp.int32)
    rel_row = jnp.where(mrow_sc[0:1, :] == grp_f, mrow_sc[1:2, :], -1.0)
    mcol = mcol_sc[...]
    rel_col = jnp.where(mcol[:, 0:1] == grp_f, mcol[:, 1:2], -1.0)
    r_iota = lax.broadcasted_iota(jnp.int32, (MOE_SUB, tm), 0).astype(F32)
    c_iota = lax.broadcasted_iota(jnp.int32, (tm, MOE_SUB), 1).astype(F32)

    def sub_tile(s, carry):
        base = (s * MOE_SUB).astype(F32)
        gather = jnp.where(r_iota + base == rel_row, 1.0, 0.0).astype(BF16)
        xs = _dot(gather, xn_sc[...]).astype(BF16)
        cw3 = _dot(gather, cw_sc[...])
        cw = cw3[:, 0:LANES] + cw3[:, LANES:2 * LANES] + cw3[:, 2 * LANES:3 * LANES]
        y = jnp.zeros((MOE_SUB, out_ref.shape[1]), F32)
        for e in range(EXPERTS_PER_GROUP):
            hcat = _dot(xs, w13_ref[e])
            hg = hcat[:, :EXPERT_FF]
            hid = hg * _sigmoid(hg) * hcat[:, EXPERT_FF:]
            y = y + _dot((hid * cw[:, e:e + 1]).astype(BF16), w2_ref[e])
        scatter = jnp.where(c_iota + base == rel_col, 1.0, 0.0).astype(BF16)
        acc_sc[...] += _dot(scatter, y.astype(BF16))
        return carry

    lax.fori_loop(0, (n_rows + MOE_SUB - 1) // MOE_SUB, sub_tile, 0)

    @pl.when(grp == N_GROUPS - 1)
    def _():
        y = h_ref[...] + acc_sc[...]
        if final_norm:
            y = y * _rms_scale(y) * fg_ref[...]
        out_ref[...] = y


def _moe(h2d, layer, ffn_g, w_group, b_group, w_router, b_router, w13_all, w2_all, final_g, final_norm):
    t, d = h2d.shape
    tm = min(MOE_TM, t)
    pad = LANES - N_GROUPS - N_EXPERTS
    wr = jnp.concatenate([w_group, w_router, jnp.zeros((d, pad), F32)], axis=1).astype(F32)
    wr_hi = wr.astype(BF16)
    wr_lo = (wr - wr_hi.astype(F32)).astype(BF16)
    br = jnp.concatenate([b_group, b_router, jnp.zeros((pad,), F32)]).astype(F32).reshape(1, LANES)
    src = jnp.arange(LANES)[:, None]
    dst = jnp.arange(LANES)[None, :]
    is_expert = jnp.logical_and(src >= ROUTER_LANE0, src < ROUTER_LANE0 + N_EXPERTS)
    lane_sel = jnp.logical_and(is_expert, (src - ROUTER_LANE0) % EXPERTS_PER_GROUP == dst).astype(BF16)
    const = lambda i, g: (0, 0)
    return pl.pallas_call(
        functools.partial(_moe_kernel, final_norm=final_norm),
        grid=(t // tm, N_GROUPS),
        in_specs=[pl.BlockSpec((tm, d), lambda i, g: (i, 0)),
                  pl.BlockSpec((1, d), const),
                  pl.BlockSpec((d, LANES), const),
                  pl.BlockSpec((d, LANES), const),
                  pl.BlockSpec((1, LANES), const),
                  pl.BlockSpec((LANES, LANES), const),
                  pl.BlockSpec((None, EXPERTS_PER_GROUP, d, 2 * EXPERT_FF), lambda i, g: (layer, g, 0, 0)),
                  pl.BlockSpec((None, EXPERTS_PER_GROUP, EXPERT_FF, d), lambda i, g: (layer, g, 0, 0)),
                  pl.BlockSpec((1, d), const)],
        out_specs=pl.BlockSpec((tm, d), lambda i, g: (i, 0)),
        out_shape=jax.ShapeDtypeStruct((t, d), F32),
        scratch_shapes=[pltpu.VMEM((tm, d), BF16),
                        pltpu.VMEM((tm, 3 * LANES), BF16),
                        pltpu.VMEM((tm, LANES), F32),
                        pltpu.VMEM((8, tm), F32),
                        pltpu.VMEM((8, LANES), F32),
                        pltpu.VMEM((tm, d), F32),
                        pltpu.VMEM((tm, tm), BF16), pltpu.VMEM((tm, tm), BF16)],
        compiler_params=pltpu.CompilerParams(dimension_semantics=("arbitrary", "arbitrary"),
                                             vmem_limit_bytes=VMEM_LIMIT),
        name="moe",
    )(h2d, ffn_g.reshape(1, d), wr_hi, wr_lo, br, lane_sel, w13_all, w2_all, final_g.reshape(1, d))


def kernel(x, mem, a_norm, a_w_in, a_conv, a_log, a_dt_bias, a_out_gain, a_w_out, kv_norm, w_kv, b_norm, b_w_in,
           b_w_out, mem_norm, w_mem_kv, ffn_norm, w_group, b_group, w_router, b_router, w1, w3, w2, final_norm):
    bsz, seq, d = x.shape
    assert seq % (SB_SUB * SB_BLOCK) == 0 and seq % GDN_TOK == 0 and seq % MIX_TS == 0, seq
    m_len = mem.shape[1]
    depth = mem_norm.shape[0]
    n_a = a_norm.shape[0]
    h = x.reshape(bsz * seq, d)
    mem2d = mem.reshape(bsz * m_len, d)
    w13_all = jnp.concatenate([w1.astype(BF16), w3.astype(BF16)], axis=-1)
    w2_all = w2.astype(BF16)
    kv = None
    for l in range(depth):
        mkv = _norm_matmul(mem2d, mem_norm[l], w_mem_kv[l], F32, PROJ_TM)
        if l < n_a:
            w_in = a_w_in[l]
            g4 = 4 * GDN_WIDTH
            w_perm = jnp.concatenate([w_in[:, :g4], w_in[:, g4 + 2 * GDN_HEADS:], w_in[:, g4:g4 + 2 * GDN_HEADS],
                                      jnp.zeros((d, LANES - 2 * GDN_HEADS), w_in.dtype)], axis=1)
            proj = _norm_matmul(h, a_norm[l], w_perm, F32, PROJ_TM)
            o = _gdn(proj, a_conv[l], a_log[l], a_dt_bias[l], a_out_gain[l], bsz, seq)
            h = _mix_out(h, o, proj, g4 // MEM_WIDTH, mkv, a_w_out[l], bsz, seq, MIX_TS)
        else:
            lb = l - n_a
            if l == n_a:
                kv, proj = _norm_matmul2(h, kv_norm, w_kv, b_norm[lb], b_w_in[lb], BF16, PROJ_TM)
            else:
                proj = _norm_matmul(h, b_norm[lb], b_w_in[lb], BF16, PROJ_TM)
            o = _stick_breaking(proj, kv, bsz, seq)
            h = _mix_out(h, o, proj, SB_WIDTH // MEM_WIDTH, mkv, b_w_out[lb], bsz, seq, MIX_TS)
        h = _moe(h, l, ffn_norm[l], w_group[l], b_group[l], w_router[l], b_router[l], w13_all, w2_all,
                 final_norm, l == depth - 1)
    return h.reshape(bsz, seq, d)
```

```python
import functools

import jax
import jax.numpy as jnp
from jax import lax
from jax.experimental import pallas as pl
from jax.experimental.pallas import tpu as pltpu

EPS = 1e-6
GDN_HEADS = 6
GDN_D = 128
GDN_WIDTH = GDN_HEADS * GDN_D
CONV_K = 4
CHUNK = 64
SB_HEADS = 12
SB_DH = 64
SB_WIDTH = SB_HEADS * SB_DH
SB_BLOCK = 128
MEM_HEADS = 4
MEM_DH = 64
MEM_WIDTH = MEM_HEADS * MEM_DH
N_GROUPS = 4
EXPERTS_PER_GROUP = 4
N_EXPERTS = N_GROUPS * EXPERTS_PER_GROUP
EXPERT_FF = 256
LANES = 128
A_IN_PAD = 4 * GDN_WIDTH + MEM_WIDTH + LANES
ROUTER_LANE0 = N_GROUPS
VMEM_LIMIT = 48 * 1024 * 1024
PROJ_TM = 512
MIX_TS = 512

F32 = jnp.float32
BF16 = jnp.bfloat16


def _dot(a, b):
    return jnp.dot(a, b, preferred_element_type=F32)


def _dot_nt(a, b):
    return lax.dot_general(a, b, (((1,), (1,)), ((), ())), preferred_element_type=F32)


def _dot_f32(a, b):
    return jnp.dot(a, b, preferred_element_type=F32, precision=lax.Precision.HIGHEST)


def _sigmoid(x):
    return 1.0 / (1.0 + jnp.exp(-x))


def _softplus(x):
    return jnp.maximum(x, 0.0) + jnp.log(1.0 + jnp.exp(-jnp.abs(x)))


def _rms_scale(x):
    return lax.rsqrt(jnp.mean(x * x, axis=-1, keepdims=True) + EPS)


def _norm_matmul_kernel(x_ref, g_ref, w_ref, o_ref, *, col_chunk):
    x = x_ref[...]
    xn = (x * _rms_scale(x) * g_ref[...]).astype(BF16)
    n = o_ref.shape[-1]
    for c0 in range(0, n, col_chunk):
        c1 = min(n, c0 + col_chunk)
        o_ref[:, c0:c1] = _dot(xn, w_ref[:, c0:c1]).astype(o_ref.dtype)


def _norm_matmul(x2d, g, w, out_dtype, tm):
    t, d = x2d.shape
    n = w.shape[1]
    tm = min(tm, t)
    return pl.pallas_call(
        functools.partial(_norm_matmul_kernel, col_chunk=4 * LANES),
        grid=(t // tm,),
        in_specs=[pl.BlockSpec((tm, d), lambda i: (i, 0)),
                  pl.BlockSpec((1, d), lambda i: (0, 0)),
                  pl.BlockSpec((d, n), lambda i: (0, 0))],
        out_specs=pl.BlockSpec((tm, n), lambda i: (i, 0)),
        out_shape=jax.ShapeDtypeStruct((t, n), out_dtype),
        compiler_params=pltpu.CompilerParams(dimension_semantics=("parallel",), vmem_limit_bytes=VMEM_LIMIT),
        name="norm_matmul",
    )(x2d, g.reshape(1, d), w.astype(BF16))


def _norm_matmul2_kernel(x_ref, ga_ref, wa_ref, gb_ref, wb_ref, oa_ref, ob_ref, *, col_chunk):
    x = x_ref[...]
    xs = x * _rms_scale(x)
    for g_ref, w_ref, o_ref in ((ga_ref, wa_ref, oa_ref), (gb_ref, wb_ref, ob_ref)):
        xn = (xs * g_ref[...]).astype(BF16)
        n = o_ref.shape[-1]
        for c0 in range(0, n, col_chunk):
            c1 = min(n, c0 + col_chunk)
            o_ref[:, c0:c1] = _dot(xn, w_ref[:, c0:c1]).astype(o_ref.dtype)


def _norm_matmul2(x2d, ga, wa, gb, wb, out_dtype, tm):
    t, d = x2d.shape
    na, nb = wa.shape[1], wb.shape[1]
    tm = min(tm, t)
    const = lambda i: (0, 0)
    return pl.pallas_call(
        functools.partial(_norm_matmul2_kernel, col_chunk=4 * LANES),
        grid=(t // tm,),
        in_specs=[pl.BlockSpec((tm, d), lambda i: (i, 0)),
                  pl.BlockSpec((1, d), const), pl.BlockSpec((d, na), const),
                  pl.BlockSpec((1, d), const), pl.BlockSpec((d, nb), const)],
        out_specs=[pl.BlockSpec((tm, na), lambda i: (i, 0)), pl.BlockSpec((tm, nb), lambda i: (i, 0))],
        out_shape=[jax.ShapeDtypeStruct((t, na), out_dtype), jax.ShapeDtypeStruct((t, nb), out_dtype)],
        compiler_params=pltpu.CompilerParams(dimension_semantics=("parallel",), vmem_limit_bytes=VMEM_LIMIT),
        name="norm_matmul2",
    )(x2d, ga.reshape(1, d), wa.astype(BF16), gb.reshape(1, d), wb.astype(BF16))


GDN_STEP_CHUNKS = 8
GDN_TOK = CHUNK * GDN_STEP_CHUNKS
CONV_TAIL = 8


def _gdn_kernel(qkv_ref, gate_ref, ba_ref, convw_ref, alog_ref, dtb_ref, ogain_ref, o_ref, xbuf, state):
    c = CHUNK
    tok = GDN_TOK

    @pl.when(pl.program_id(1) == 0)
    def _():
        xbuf[0:CONV_TAIL, :] = jnp.zeros((CONV_TAIL, xbuf.shape[1]), F32)
        state[...] = jnp.zeros(state.shape, F32)

    xbuf[CONV_TAIL:CONV_TAIL + tok, :] = qkv_ref[...]

    def conv_act(lo):
        win = xbuf[:, lo:lo + GDN_D]
        acc = win[CONV_TAIL:] * convw_ref[CONV_K - 1:CONV_K, lo:lo + GDN_D]
        for j in range(CONV_K - 1):
            shifted = pltpu.roll(win, CONV_K - 1 - j, axis=0)[CONV_TAIL:]
            acc = acc + shifted * convw_ref[j:j + 1, lo:lo + GDN_D]
        return acc * _sigmoid(acc)

    ba = ba_ref[...]
    beta_all = _sigmoid(ba)
    g_all = -jnp.exp(alog_ref[...]) * _softplus(ba + dtb_ref[...])
    trow = lax.broadcasted_iota(jnp.int32, (tok, tok), 0)
    tcol = lax.broadcasted_iota(jnp.int32, (tok, tok), 1)
    same_chunk = (trow // c) == (tcol // c)
    lower = jnp.where(jnp.logical_and(same_chunk, trow >= tcol), 1.0, 0.0)
    upper = jnp.where(jnp.logical_and(same_chunk, trow <= tcol), 1.0, 0.0)
    gc_all = _dot_f32(lower, g_all)
    gct_all = _dot_f32(g_all.T, upper)
    row = lax.broadcasted_iota(jnp.int32, (c, c), 0)
    col = lax.broadcasted_iota(jnp.int32, (c, c), 1)
    incl = row >= col
    strict = row > col
    eye = (row == col).astype(F32)
    ogain = ogain_ref[...]

    heads = range(GDN_HEADS)
    chunks = range(GDN_STEP_CHUNKS)
    items = [(ci, h) for ci in chunks for h in heads]

    q_full, k_full, v_full = [], [], []
    for h in heads:
        lo = h * GDN_D
        qf = conv_act(lo)
        kf = conv_act(GDN_WIDTH + lo)
        q_full.append(qf * lax.rsqrt(jnp.sum(qf * qf, axis=-1, keepdims=True) + EPS) * (GDN_D ** -0.5))
        k_full.append(kf * lax.rsqrt(jnp.sum(kf * kf, axis=-1, keepdims=True) + EPS))
        v_full.append(conv_act(2 * GDN_WIDTH + lo))

    pre = {}
    for ci, h in items:
        r0 = ci * c
        gl = GDN_HEADS + h
        qh = q_full[h][r0:r0 + c]
        kh = k_full[h][r0:r0 + c]
        vh = v_full[h][r0:r0 + c]
        beta = beta_all[r0:r0 + c, h:h + 1]
        gcol = gc_all[r0:r0 + c, gl:gl + 1]
        grow = gct_all[gl:gl + 1, r0:r0 + c]
        glast = gc_all[r0 + c - 1:r0 + c, gl:gl + 1]
        decay = jnp.where(incl, jnp.exp(jnp.where(incl, gcol - grow, 0.0)), 0.0)
        kb = kh * beta
        khb = kh.astype(BF16)
        egc = jnp.exp(gcol)
        pre[ci, h] = dict(
            p=-jnp.where(strict, _dot_nt(kb.astype(BF16), khb) * decay, 0.0),
            att=jnp.where(incl, _dot_nt(qh.astype(BF16), khb) * decay, 0.0).astype(BF16),
            rhs=jnp.concatenate([vh * beta, kb * egc], axis=1).astype(BF16),
            qg=qh * egc,
            ktail_t=(kh * jnp.exp(glast - gcol)).T.astype(BF16),
            sdecay=jnp.exp(glast))

    tinv = {it: eye + pre[it]["p"] for it in items}
    pw = {it: pre[it]["p"] for it in items}
    for _ in range(5):
        for it in items:
            pb = pw[it].astype(BF16)
            pw[it] = _dot(pb, pb)
        for it in items:
            tinv[it] = tinv[it] + _dot(tinv[it].astype(BF16), pw[it].astype(BF16))
    sol = {it: _dot(tinv[it].astype(BF16), pre[it]["rhs"]) for it in items}

    drow = lax.broadcasted_iota(jnp.int32, (GDN_D, GDN_D), 0)
    dcol = lax.broadcasted_iota(jnp.int32, (GDN_D, GDN_D), 1)
    eye_d = (drow == dcol).astype(F32)
    lin = {}
    for it in items:
        pr = pre[it]
        solb = sol[it].astype(BF16)
        att_sol = _dot(pr["att"], solb)
        kt_sol = _dot(pr["ktail_t"], solb)
        lin[it] = ((pr["qg"] - att_sol[:, GDN_D:]).astype(BF16), att_sol[:, :GDN_D],
                   (eye_d * pr["sdecay"] - kt_sol[:, GDN_D:]).astype(BF16), kt_sol[:, :GDN_D])
    s_cur = [state[h] for h in heads]
    for ci in chunks:
        r0 = ci * c
        for h in heads:
            q_lin, o_const, s_lin, s_const = lin[ci, h]
            lo = h * GDN_D
            sb = s_cur[h].astype(BF16)
            o = _dot(q_lin, sb) + o_const
            s_cur[h] = _dot(s_lin, sb) + s_const
            gate = gate_ref[r0:r0 + c, lo:lo + GDN_D]
            o = o * _rms_scale(o) * ogain * (gate * _sigmoid(gate))
            o_ref[r0:r0 + c, lo:lo + GDN_D] = o.astype(o_ref.dtype)
    for h in heads:
        state[h] = s_cur[h]
    xbuf[0:CONV_TAIL, :] = xbuf[tok:tok + CONV_TAIL, :]


def _gdn(proj, conv_w, a_log, dt_bias, o_gain, bsz, seq):
    t = proj.shape[0]
    n_s = seq // GDN_TOK
    qkv_w = 3 * GDN_WIDTH
    pad = jnp.zeros((GDN_HEADS,), F32)
    alog_row = jnp.concatenate([pad, a_log.astype(F32), jnp.zeros((LANES - 2 * GDN_HEADS,), F32)]).reshape(1, LANES)
    dtb_row = jnp.concatenate([pad, dt_bias.astype(F32), jnp.zeros((LANES - 2 * GDN_HEADS,), F32)]).reshape(1, LANES)
    return pl.pallas_call(
        _gdn_kernel,
        grid=(bsz, n_s),
        in_specs=[pl.BlockSpec((GDN_TOK, qkv_w), lambda b, s: (b * n_s + s, 0)),
                  pl.BlockSpec((GDN_TOK, GDN_WIDTH), lambda b, s: (b * n_s + s, qkv_w // GDN_WIDTH)),
                  pl.BlockSpec((GDN_TOK, LANES), lambda b, s: (b * n_s + s, (A_IN_PAD - LANES) // LANES)),
                  pl.BlockSpec((CONV_K, qkv_w), lambda b, s: (0, 0)),
                  pl.BlockSpec((1, LANES), lambda b, s: (0, 0)),
                  pl.BlockSpec((1, LANES), lambda b, s: (0, 0)),
                  pl.BlockSpec((1, GDN_D), lambda b, s: (0, 0))],
        out_specs=pl.BlockSpec((GDN_TOK, GDN_WIDTH), lambda b, s: (b * n_s + s, 0)),
        out_shape=jax.ShapeDtypeStruct((t, GDN_WIDTH), BF16),
        scratch_shapes=[pltpu.VMEM((GDN_TOK + CONV_TAIL, qkv_w), F32),
                        pltpu.VMEM((GDN_HEADS, GDN_D, GDN_D), F32)],
        compiler_params=pltpu.CompilerParams(dimension_semantics=("parallel", "arbitrary"),
                                             vmem_limit_bytes=VMEM_LIMIT),
        name="gdn",
    )(proj, proj, proj, conv_w.astype(F32), alog_row, dtb_row, o_gain.astype(F32).reshape(1, GDN_D))


def _mix_out_kernel(h_ref, o_ref, mq_ref, mk_ref, mv_ref, wo_ref, wm_ref, out_ref):
    mq = (mq_ref[...] * (MEM_DH ** -0.5)).astype(BF16)
    mk = mk_ref[...]
    mv = mv_ref[...]
    head = lax.broadcasted_iota(jnp.int32, mk.shape, 1) // MEM_DH
    heads = range(MEM_HEADS)
    k_heads = [jnp.where(head == hh, mk, 0.0).astype(BF16) for hh in heads]
    v_heads = [jnp.where(head == hh, mv, 0.0).astype(BF16) for hh in heads]
    scores = [_dot_nt(mq, k_heads[hh]) for hh in heads]
    probs = []
    for sc in scores:
        p = jnp.exp(sc - jnp.max(sc, axis=-1, keepdims=True))
        probs.append((p * (1.0 / jnp.sum(p, axis=-1, keepdims=True))).astype(BF16))
    m = _dot(probs[0], v_heads[0])
    for hh in heads[1:]:
        m = m + _dot(probs[hh], v_heads[hh])
    y = _dot(o_ref[...].astype(BF16), wo_ref[...]) + _dot(m.astype(BF16), wm_ref[...])
    out_ref[...] = h_ref[...] + y


def _mix_out(h2d, o, proj, mq_block, mkv, w_out, bsz, seq, ts):
    t, d = h2d.shape
    ts = min(ts, seq)
    n_s = seq // ts
    m_len = mkv.shape[0] // bsz
    ow = o.shape[1]
    return pl.pallas_call(
        _mix_out_kernel,
        grid=(bsz, n_s),
        in_specs=[pl.BlockSpec((ts, d), lambda b, s: (b * n_s + s, 0)),
                  pl.BlockSpec((ts, ow), lambda b, s: (b * n_s + s, 0)),
                  pl.BlockSpec((ts, MEM_WIDTH), lambda b, s: (b * n_s + s, mq_block)),
                  pl.BlockSpec((m_len, MEM_WIDTH), lambda b, s: (b, 0)),
                  pl.BlockSpec((m_len, MEM_WIDTH), lambda b, s: (b, 1)),
                  pl.BlockSpec((ow, d), lambda b, s: (0, 0)),
                  pl.BlockSpec((MEM_WIDTH, d), lambda b, s: (0, 0))],
        out_specs=pl.BlockSpec((ts, d), lambda b, s: (b * n_s + s, 0)),
        out_shape=jax.ShapeDtypeStruct((t, d), F32),
        compiler_params=pltpu.CompilerParams(dimension_semantics=("parallel", "parallel"),
                                             vmem_limit_bytes=VMEM_LIMIT),
        name="mix_out",
    )(h2d, o, proj, mkv, mkv, w_out[:ow].astype(BF16), w_out[ow:].astype(BF16))


SB_SUB = 8
SB_NEG = -1e30
SB_DONE = 88.0
SB_HEAD_ROWS = 16


def _split_bf16(x):
    hi = pltpu.bitcast(pltpu.bitcast(x, jnp.uint32) & jnp.uint32(0xFFFF0000), F32)
    return hi.astype(BF16), (x - hi).astype(BF16)


def _sb_kernel(q_ref, k_ref, v_ref, o_ref):
    blk = SB_BLOCK
    step = pl.program_id(2)
    lane = lax.broadcasted_iota(jnp.int32, (blk, 2 * SB_DH), 1)
    row1 = lax.broadcasted_iota(jnp.int32, (blk, blk), 0)
    col1 = lax.broadcasted_iota(jnp.int32, (blk, blk), 1)
    causal = col1 < row1
    srow2 = lax.broadcasted_iota(jnp.int32, (2 * blk, 2 * blk), 0)
    scol2 = lax.broadcasted_iota(jnp.int32, (2 * blk, 2 * blk), 1)
    suffix2 = (srow2 >= scol2).astype(BF16)
    suffix1 = suffix2[:blk, :blk]
    suffix2x2 = jnp.concatenate([suffix2, suffix2], axis=0)
    suffix1x2 = jnp.concatenate([suffix1, suffix1], axis=0)

    def suffix_sums(sp, suffix_x2):
        return _dot(jnp.concatenate(_split_bf16(sp), axis=1), suffix_x2)

    q_heads, k2s, v2s = [], [], []
    for j in range(SB_SUB):
        i = step * SB_SUB + j
        q = q_ref[j * blk:(j + 1) * blk, :] * (SB_DH ** -0.5)
        q_heads.append(jnp.where(lane < SB_DH, q, 0.0).astype(BF16))
        q_heads.append(jnp.where(lane >= SB_DH, q, 0.0).astype(BF16))
        prev = pl.multiple_of(jnp.maximum(i - 1, 0) * blk, blk)
        diag = pl.multiple_of(i * blk, blk)
        k2s.append(jnp.concatenate([k_ref[pl.ds(prev, blk), :], k_ref[pl.ds(diag, blk), :]], axis=0))
        has_prev = jnp.where(i > 0, 1.0, 0.0).astype(BF16)
        v2s.append(jnp.concatenate([v_ref[pl.ds(prev, blk), :] * has_prev, v_ref[pl.ds(diag, blk), :]], axis=0))
    def older_block(offset):
        kblks, vblks = [], []
        for j in range(SB_SUB):
            kb = step * SB_SUB + j - offset
            start = pl.multiple_of(jnp.maximum(kb, 0) * blk, blk)
            kblks.append(k_ref[pl.ds(start, blk), :])
            vblks.append(v_ref[pl.ds(start, blk), :] * jnp.where(kb >= 0, 1.0, 0.0).astype(BF16))
        return kblks, vblks

    hr = SB_HEAD_ROWS
    k3s, v3s = older_block(2)
    chains = range(2 * SB_SUB)
    raw = [_dot_nt(q_heads[n], k2s[n // 2]) for n in chains]
    z3 = [_dot_nt(q_heads[n][:hr], k3s[n // 2]) for n in chains]
    zs = [jnp.concatenate([raw[n][:, :blk], jnp.where(causal, raw[n][:, blk:], SB_NEG)], axis=1) for n in chains]
    sums = [suffix_sums(_softplus(zs[n]), suffix2x2) for n in chains]
    s3 = [suffix_sums(_softplus(z3[n]), suffix1x2) for n in chains]
    probs = [jnp.exp(zs[n] - sums[n]).astype(BF16) for n in chains]
    carries = [sums[n][:, 0:1] for n in chains]
    p3 = [jnp.exp(z3[n] - carries[n][:hr] - s3[n]).astype(BF16) for n in chains]
    accs = [_dot(probs[n], v2s[n // 2]) for n in chains]
    accs = [jnp.concatenate([accs[n][:hr] + _dot(p3[n], v3s[n // 2]), accs[n][hr:]], axis=0) for n in chains]
    carries = [jnp.concatenate([carries[n][:hr] + s3[n][:, 0:1], carries[n][hr:]], axis=0) for n in chains]
    head_done = row1 < hr

    def lowest(cs):
        m = cs[0]
        for cc in cs[1:]:
            m = jnp.minimum(m, cc)
        return jnp.min(m)

    def more(st):
        it, cmin = st[0], st[1]
        return jnp.logical_and(it < (step + 1) * SB_SUB - 2, cmin < SB_DONE)

    def body(st):
        it = st[0]
        cs = list(st[2:2 + 2 * SB_SUB])
        acs = list(st[2 + 2 * SB_SUB:])
        kblks, vblks = older_block(2 + it)
        skip = jnp.logical_and(it == 0, head_done)
        zl = [jnp.where(skip, SB_NEG, _dot_nt(q_heads[n], kblks[n // 2])) for n in chains]
        sl = [suffix_sums(_softplus(zl[n]), suffix1x2) for n in chains]
        pl_ = [jnp.exp(zl[n] - cs[n] - sl[n]).astype(BF16) for n in chains]
        acs = [acs[n] + _dot(pl_[n], vblks[n // 2]) for n in chains]
        cs = [cs[n] + sl[n][:, 0:1] for n in chains]
        return (it + 1, lowest(cs), *cs, *acs)

    st = lax.while_loop(more, body, (jnp.int32(0), lowest(carries), *carries, *accs))
    accs = st[2 + 2 * SB_SUB:]
    for j in range(SB_SUB):
        o_ref[j * blk:(j + 1) * blk, :] = jnp.where(lane < SB_DH, accs[2 * j], accs[2 * j + 1]).astype(o_ref.dtype)


def _stick_breaking(qproj, kv, bsz, seq):
    t = qproj.shape[0]
    tq = SB_SUB * SB_BLOCK
    n_q = seq // tq
    pairs = SB_HEADS // 2
    pw = 2 * SB_DH
    return pl.pallas_call(
        _sb_kernel,
        grid=(bsz, pairs, n_q),
        in_specs=[pl.BlockSpec((tq, pw), lambda b, p, i: (b * n_q + i, p)),
                  pl.BlockSpec((seq, pw), lambda b, p, i: (b, p)),
                  pl.BlockSpec((seq, pw), lambda b, p, i: (b, pairs + p))],
        out_specs=pl.BlockSpec((tq, pw), lambda b, p, i: (b * n_q + i, p)),
        out_shape=jax.ShapeDtypeStruct((t, SB_WIDTH), BF16),
        compiler_params=pltpu.CompilerParams(dimension_semantics=("parallel", "parallel", "arbitrary"),
                                             vmem_limit_bytes=VMEM_LIMIT),
        name="stick_breaking",
    )(qproj, kv, kv)


MOE_TM = 1024
MOE_SUB = 320


def _split3(x):
    hi = x.astype(BF16)
    r1 = x - hi.astype(F32)
    mid = r1.astype(BF16)
    return hi, mid, (r1 - mid.astype(F32)).astype(BF16)


def _moe_kernel(h_ref, g_ref, wrh_ref, wrl_ref, br_ref, sel_ref, w13_ref, w2_ref, fg_ref, out_ref,
                xn_sc, cw_sc, mcol_sc, mrow_sc, cnt_sc, acc_sc, tril_sc, triu_sc, *, final_norm):
    tile = pl.program_id(0)
    grp = pl.program_id(1)
    tm = h_ref.shape[0]

    @pl.when(jnp.logical_and(tile == 0, grp == 0))
    def _():
        r = lax.broadcasted_iota(jnp.int32, (tm, tm), 0)
        c = lax.broadcasted_iota(jnp.int32, (tm, tm), 1)
        tril_sc[...] = jnp.where(r > c, 1.0, 0.0).astype(BF16)
        triu_sc[...] = jnp.where(r < c, 1.0, 0.0).astype(BF16)

    @pl.when(grp == 0)
    def _():
        x = h_ref[...]
        xn = x * _rms_scale(x) * g_ref[...]
        xh = xn.astype(BF16)
        xn_sc[...] = xh
        xl = (xn - xh.astype(F32)).astype(BF16)
        logits = _dot(xh, wrh_ref[...]) + _dot(xl, wrh_ref[...]) + _dot(xh, wrl_ref[...]) + br_ref[...]
        lane = lax.broadcasted_iota(jnp.int32, (tm, LANES), 1)
        lane_f = lane.astype(F32)
        big = float(LANES)
        is_g = lane < N_GROUPS
        gl = jnp.where(is_g, logits, -jnp.inf)
        gmax = jnp.max(gl, axis=-1, keepdims=True)
        p_group = 1.0 / jnp.sum(jnp.where(is_g, jnp.exp(logits - gmax), 0.0), axis=-1, keepdims=True)
        gsel = jnp.min(jnp.where(gl == gmax, lane_f, big), axis=-1, keepdims=True)
        e_lane = lane - ROUTER_LANE0
        grp_of_lane = (e_lane // EXPERTS_PER_GROUP).astype(F32)
        in_grp = jnp.logical_and(jnp.logical_and(e_lane >= 0, e_lane < N_EXPERTS), grp_of_lane == gsel)
        el = jnp.where(in_grp, logits, -jnp.inf)
        emax = jnp.max(el, axis=-1, keepdims=True)
        ee = jnp.where(in_grp, jnp.exp(logits - emax), 0.0)
        prob = ee / jnp.sum(ee, axis=-1, keepdims=True)
        pm = jnp.where(in_grp, prob, -1.0)
        p1 = jnp.max(pm, axis=-1, keepdims=True)
        i1 = jnp.min(jnp.where(pm == p1, lane_f, big), axis=-1, keepdims=True)
        pm2 = jnp.where(lane_f == i1, -1.0, pm)
        p2 = jnp.max(pm2, axis=-1, keepdims=True)
        i2 = jnp.min(jnp.where(pm2 == p2, lane_f, big), axis=-1, keepdims=True)
        sel = jnp.logical_or(lane_f == i1, lane_f == i2)
        comb = jnp.where(sel, p_group * (prob / (p1 + p2)), 0.0)
        ch, cm, cl = _split3(comb)
        lane_sel = sel_ref[...]
        comb4 = _dot(ch, lane_sel) + _dot(cm, lane_sel) + _dot(cl, lane_sel)
        c4h, c4m, c4l = _split3(comb4)
        cw_sc[:, 0:LANES] = c4h
        cw_sc[:, LANES:2 * LANES] = c4m
        cw_sc[:, 2 * LANES:3 * LANES] = c4l
        onehot = jnp.where(lane_f == gsel, 1.0, 0.0)
        rank_col = jnp.sum(onehot * _dot(tril_sc[...], onehot.astype(BF16)), axis=-1, keepdims=True)
        mcol_sc[...] = jnp.where(lane == 0, gsel, jnp.where(lane == 1, rank_col, 0.0))
        cnt_sc[...] = jnp.broadcast_to(jnp.sum(onehot, axis=0, keepdims=True), cnt_sc.shape)
        oht = onehot.T
        cumt = _dot(oht.astype(BF16), triu_sc[...])
        sub = lax.broadcasted_iota(jnp.int32, (LANES, tm), 0)
        rank_row = jnp.sum(oht * cumt, axis=0, keepdims=True)
        grp_row = jnp.sum(oht * sub.astype(F32), axis=0, keepdims=True)
        srow = lax.broadcasted_iota(jnp.int32, mrow_sc.shape, 0)
        mrow_sc[...] = jnp.where(srow == 0, grp_row, jnp.where(srow == 1, rank_row, 0.0))
        acc_sc[...] = jnp.zeros(acc_sc.shape, F32)

    grp_f = grp.astype(F32)
    cnt_lane = lax.broadcasted_iota(jnp.int32, (1, LANES), 1)
    n_rows = jnp.sum(jnp.where(cnt_lane == grp, cnt_sc[0:1, :], 0.0)).astype(jnp.int32)
    rel_row = jnp.where(mrow_sc[0:1, :] == grp_f, mrow_sc[1:2, :], -1.0)
    mcol = mcol_sc[...]
    rel_col = jnp.where(mcol[:, 0:1] == grp_f, mcol[:, 1:2], -1.0)
    r_iota = lax.broadcasted_iota(jnp.int32, (MOE_SUB, tm), 0).astype(F32)
    c_iota = lax.broadcasted_iota(jnp.int32, (tm, MOE_SUB), 1).astype(F32)

    def sub_tile(s, carry):
        base = (s * MOE_SUB).astype(F32)
        gather = jnp.where(r_iota + base == rel_row, 1.0, 0.0).astype(BF16)
        xs = _dot(gather, xn_sc[...]).astype(BF16)
        cw3 = _dot(gather, cw_sc[...])
        cw = cw3[:, 0:LANES] + cw3[:, LANES:2 * LANES] + cw3[:, 2 * LANES:3 * LANES]
        y = jnp.zeros((MOE_SUB, out_ref.shape[1]), F32)
        for e in range(EXPERTS_PER_GROUP):
            hcat = _dot(xs, w13_ref[e])
            hg = hcat[:, :EXPERT_FF]
            hid = hg * _sigmoid(hg) * hcat[:, EXPERT_FF:]
            y = y + _dot((hid * cw[:, e:e + 1]).astype(BF16), w2_ref[e])
        scatter = jnp.where(c_iota + base == rel_col, 1.0, 0.0).astype(BF16)
        acc_sc[...] += _dot(scatter, y.astype(BF16))
        return carry

    lax.fori_loop(0, (n_rows + MOE_SUB - 1) // MOE_SUB, sub_tile, 0)

    @pl.when(grp == N_GROUPS - 1)
    def _():
        y = h_ref[...] + acc_sc[...]
        if final_norm:
            y = y * _rms_scale(y) * fg_ref[...]
        out_ref[...] = y


def _moe(h2d, layer, ffn_g, w_group, b_group, w_router, b_router, w13_all, w2_all, final_g, final_norm):
    t, d = h2d.shape
    tm = min(MOE_TM, t)
    pad = LANES - N_GROUPS - N_EXPERTS
    wr = jnp.concatenate([w_group, w_router, jnp.zeros((d, pad), F32)], axis=1).astype(F32)
    wr_hi = wr.astype(BF16)
    wr_lo = (wr - wr_hi.astype(F32)).astype(BF16)
    br = jnp.concatenate([b_group, b_router, jnp.zeros((pad,), F32)]).astype(F32).reshape(1, LANES)
    src = jnp.arange(LANES)[:, None]
    dst = jnp.arange(LANES)[None, :]
    is_expert = jnp.logical_and(src >= ROUTER_LANE0, src < ROUTER_LANE0 + N_EXPERTS)
    lane_sel = jnp.logical_and(is_expert, (src - ROUTER_LANE0) % EXPERTS_PER_GROUP == dst).astype(BF16)
    const = lambda i, g: (0, 0)
    return pl.pallas_call(
        functools.partial(_moe_kernel, final_norm=final_norm),
        grid=(t // tm, N_GROUPS),
        in_specs=[pl.BlockSpec((tm, d), lambda i, g: (i, 0)),
                  pl.BlockSpec((1, d), const),
                  pl.BlockSpec((d, LANES), const),
                  pl.BlockSpec((d, LANES), const),
                  pl.BlockSpec((1, LANES), const),
                  pl.BlockSpec((LANES, LANES), const),
                  pl.BlockSpec((None, EXPERTS_PER_GROUP, d, 2 * EXPERT_FF), lambda i, g: (layer, g, 0, 0)),
                  pl.BlockSpec((None, EXPERTS_PER_GROUP, EXPERT_FF, d), lambda i, g: (layer, g, 0, 0)),
                  pl.BlockSpec((1, d), const)],
        out_specs=pl.BlockSpec((tm, d), lambda i, g: (i, 0)),
        out_shape=jax.ShapeDtypeStruct((t, d), F32),
        scratch_shapes=[pltpu.VMEM((tm, d), BF16),
                        pltpu.VMEM((tm, 3 * LANES), BF16),
                        pltpu.VMEM((tm, LANES), F32),
                        pltpu.VMEM((8, tm), F32),
                        pltpu.VMEM((8, LANES), F32),
                        pltpu.VMEM((tm, d), F32),
                        pltpu.VMEM((tm, tm), BF16), pltpu.VMEM((tm, tm), BF16)],
        compiler_params=pltpu.CompilerParams(dimension_semantics=("arbitrary", "arbitrary"),
                                             vmem_limit_bytes=VMEM_LIMIT),
        name="moe",
    )(h2d, ffn_g.reshape(1, d), wr_hi, wr_lo, br, lane_sel, w13_all, w2_all, final_g.reshape(1, d))


def kernel(x, mem, a_norm, a_w_in, a_conv, a_log, a_dt_bias, a_out_gain, a_w_out, kv_norm, w_kv, b_norm, b_w_in,
           b_w_out, mem_norm, w_mem_kv, ffn_norm, w_group, b_group, w_router, b_router, w1, w3, w2, final_norm):
    bsz, seq, d = x.shape
    assert seq % (SB_SUB * SB_BLOCK) == 0 and seq % GDN_TOK == 0 and seq % MIX_TS == 0, seq
    m_len = mem.shape[1]
    depth = mem_norm.shape[0]
    n_a = a_norm.shape[0]
    h = x.reshape(bsz * seq, d)
    mem2d = mem.reshape(bsz * m_len, d)
    w13_all = jnp.concatenate([w1.astype(BF16), w3.astype(BF16)], axis=-1)
    w2_all = w2.astype(BF16)
    kv = None
    for l in range(depth):
        mkv = _norm_matmul(mem2d, mem_norm[l], w_mem_kv[l], F32, PROJ_TM)
        if l < n_a:
            w_in = a_w_in[l]
            g4 = 4 * GDN_WIDTH
            w_perm = jnp.concatenate([w_in[:, :g4], w_in[:, g4 + 2 * GDN_HEADS:], w_in[:, g4:g4 + 2 * GDN_HEADS],
                                      jnp.zeros((d, LANES - 2 * GDN_HEADS), w_in.dtype)], axis=1)
            proj = _norm_matmul(h, a_norm[l], w_perm, F32, PROJ_TM)
            o = _gdn(proj, a_conv[l], a_log[l], a_dt_bias[l], a_out_gain[l], bsz, seq)
            h = _mix_out(h, o, proj, g4 // MEM_WIDTH, mkv, a_w_out[l], bsz, seq, MIX_TS)
        else:
            lb = l - n_a
            if l == n_a:
                kv, proj = _norm_matmul2(h, kv_norm, w_kv, b_norm[lb], b_w_in[lb], BF16, PROJ_TM)
            else:
                proj = _norm_matmul(h, b_norm[lb], b_w_in[lb], BF16, PROJ_TM)
            o = _stick_breaking(proj, kv, bsz, seq)
            h = _mix_out(h, o, proj, SB_WIDTH // MEM_WIDTH, mkv, b_w_out[lb], bsz, seq, MIX_TS)
        h = _moe(h, l, ffn_norm[l], w_group[l], b_group[l], w_router[l], b_router[l], w13_all, w2_all,
                 final_norm, l == depth - 1)
    return h.reshape(bsz, seq, d)
```
